```python
import math
import jax, jax.numpy as jnp
from jax import lax
import numpy as np

D_MODEL = 4096
BATCH = 1
SEQ = 16384
DEPTH = 1
DEC_BATCH = 4
DEC_SEQ = 2048
PAST_LEN = 128

CONV_CH = 2048
CONV_WIDTH = 31
CONV_HALF = CONV_WIDTH // 2
HEAD_DIM = 128
N_SLOTS = 8
DILATED_GROUPS = ((128, 1), (512, 4), (2048, 16))
N_GROUPS = len(DILATED_GROUPS)
N_ATTN_HEADS = N_SLOTS * N_GROUPS
ATTN_W = N_ATTN_HEADS * HEAD_DIM
ATTN_OUT_W = N_SLOTS * HEAD_DIM
ROPE_DIM = HEAD_DIM // 4
ROPE_THETA = 500000.0
D_FF = ((8 * D_MODEL + 3 * 256 - 1) // (3 * 256)) * 256
PLE_DIM = 256
ALPHA = (2.0 * DEPTH) ** 0.25
BETA = (8.0 * DEPTH) ** -0.25
LN_EPS = 1e-5
NEG_INF = -1e30
W_IN_COLS = 2 * CONV_CH + 3 * ATTN_W + 2 * D_MODEL

kernel_name = "hybrid_conformer_dilated_attn_encoder"


def layer_norm(x, g, b):
    xf = x.astype(jnp.float32)
    mu = jnp.mean(xf, axis=-1, keepdims=True)
    var = jnp.mean(jnp.square(xf - mu), axis=-1, keepdims=True)
    return ((xf - mu) * lax.rsqrt(var + LN_EPS) * g.astype(jnp.float32) + b.astype(jnp.float32)).astype(x.dtype)


def partial_rope(t, pos):
    half = ROPE_DIM // 2
    inv = ROPE_THETA ** (-jnp.arange(0, ROPE_DIM, 2, dtype=jnp.float32) / ROPE_DIM)
    ang = pos[:, None] * inv[None, :]
    cos = jnp.cos(ang)[None, :, None, :]
    sin = jnp.sin(ang)[None, :, None, :]
    tf = t[..., :ROPE_DIM].astype(jnp.float32)
    t1, t2 = tf[..., :half], tf[..., half:]
    rot = jnp.concatenate([t1 * cos - t2 * sin, t2 * cos + t1 * sin], axis=-1).astype(t.dtype)
    return jnp.concatenate([rot, t[..., ROPE_DIM:]], axis=-1)


def dilated_band_attention(q, k, v, window, dilation):
    B, S, H, Dh = q.shape
    radius = window // (2 * dilation)
    blk = radius
    L = S // dilation
    nb = -(-L // blk)
    Lp = nb * blk

    def to_sub(t):
        return t.reshape(B, L, dilation, H, Dh).transpose(0, 2, 1, 3, 4)

    qs = jnp.pad(to_sub(q), ((0, 0), (0, 0), (0, Lp - L), (0, 0), (0, 0)))
    qs = qs.reshape(B, dilation, nb, blk, H, Dh)
    pad_kv = ((0, 0), (0, 0), (blk, Lp - L + blk), (0, 0), (0, 0))

    def band(t):
        t = t.reshape(B, dilation, nb + 2, blk, H, Dh)
        return jnp.concatenate([t[:, :, :-2], t[:, :, 1:-1], t[:, :, 2:]], axis=3)

    kb = band(jnp.pad(to_sub(k), pad_kv))
    vb = band(jnp.pad(to_sub(v), pad_kv))

    scale = 1.0 / math.sqrt(Dh)
    s = jnp.einsum('brnqhd,brnkhd->brnhqk', qs, kb).astype(jnp.float32) * scale
    n_idx = jnp.arange(nb)[:, None, None]
    i_idx = jnp.arange(blk)[None, :, None]
    t_idx = jnp.arange(3 * blk)[None, None, :]
    lk = (n_idx - 1) * blk + t_idx
    valid = (jnp.abs(t_idx - blk - i_idx) <= radius) & (lk >= 0) & (lk < L)
    s = jnp.where(valid[None, None, :, None, :, :], s, NEG_INF)
    m = jnp.max(s, axis=-1, keepdims=True)
    e = jnp.exp(s - m)
    den = jnp.sum(e, axis=-1, keepdims=True)
    o = jnp.einsum('brnhqk,brnkhd->brnqhd', (e / den).astype(v.dtype), vb)
    lse = (m + jnp.log(den))[..., 0]

    o = o.reshape(B, dilation, Lp, H, Dh)[:, :, :L]
    o = o.transpose(0, 2, 1, 3, 4).reshape(B, S, H, Dh)
    lse = lse.transpose(0, 1, 2, 4, 3).reshape(B, dilation, Lp, H)[:, :, :L]
    lse = lse.transpose(0, 2, 1, 3).reshape(B, S, H)
    return o, lse


def token_mixer(x, w_in, conv_w, conv_b, conv_ln_g, conv_ln_b, w_conv_out, w_attn_out, w_out):
    B, S, _ = x.shape
    h = x @ w_in
    o1 = 2 * CONV_CH
    o2 = o1 + ATTN_W
    o3 = o2 + ATTN_W
    o4 = o3 + ATTN_W
    o5 = o4 + D_MODEL
    glu_a, glu_b = h[..., :CONV_CH], h[..., CONV_CH:o1]
    q = h[..., o1:o2].reshape(B, S, N_ATTN_HEADS, HEAD_DIM)
    k = h[..., o2:o3].reshape(B, S, N_ATTN_HEADS, HEAD_DIM)
    v = h[..., o3:o4].reshape(B, S, N_ATTN_HEADS, HEAD_DIM)
    gate_conv = jax.nn.sigmoid(h[..., o4:o5])
    gate_attn = jax.nn.sigmoid(h[..., o5:])

    u = glu_a * jax.nn.sigmoid(glu_b)
    u = lax.conv_general_dilated(u, conv_w, window_strides=(1,), padding=[(CONV_HALF, CONV_HALF)],
                                 dimension_numbers=('NWC', 'WIO', 'NWC'),
                                 feature_group_count=CONV_CH) + conv_b
    u = jax.nn.silu(layer_norm(u, conv_ln_g, conv_ln_b))
    conv_branch = u @ w_conv_out

    pos = jnp.arange(S, dtype=jnp.float32)
    q = partial_rope(q, pos)
    k = partial_rope(k, pos)
    outs, lses = [], []
    for g, (window, dilation) in enumerate(DILATED_GROUPS):
        sl = slice(g * N_SLOTS, (g + 1) * N_SLOTS)
        o_g, lse_g = dilated_band_attention(q[:, :, sl], k[:, :, sl], v[:, :, sl], window, dilation)
        outs.append(o_g)
        lses.append(lse_g)
    wts = jax.nn.softmax(jnp.stack(lses, axis=0), axis=0)
    attn = jnp.einsum('gbsh,gbshd->bshd', wts.astype(x.dtype), jnp.stack(outs, axis=0))
    attn_branch = attn.reshape(B, S, ATTN_OUT_W) @ w_attn_out

    merged = gate_conv * conv_branch + gate_attn * attn_branch
    return merged @ w_out


def encoder_layer(x, p, w_in, conv_w, conv_b, conv_ln_g, conv_ln_b, w_conv_out, w_attn_out, w_out,
                  ln1_g, ln1_b, w_gate_up, w_down, w_ple_gate, w_ple_proj, ln2_g, ln2_b):
    mix = token_mixer(x, w_in, conv_w, conv_b, conv_ln_g, conv_ln_b, w_conv_out, w_attn_out, w_out)
    x1 = layer_norm(ALPHA * x + mix, ln1_g, ln1_b)
    gu = x1 @ w_gate_up
    ffn = (jax.nn.silu(gu[..., :D_FF]) * gu[..., D_FF:]) @ w_down
    ple = jax.nn.sigmoid(x1 @ w_ple_gate) * (p @ w_ple_proj)
    return layer_norm(ALPHA * x1 + ffn + ple, ln2_g, ln2_b)


def trunk(x, p, w_in, conv_w, conv_b, conv_ln_g, conv_ln_b, w_conv_out, w_attn_out, w_out,
          ln1_g, ln1_b, w_gate_up, w_down, w_ple_gate, w_ple_proj, ln2_g, ln2_b):
    for i in range(DEPTH):
        x = encoder_layer(x, p[i], w_in[i], conv_w[i], conv_b[i], conv_ln_g[i], conv_ln_b[i],
                          w_conv_out[i], w_attn_out[i], w_out[i], ln1_g[i], ln1_b[i],
                          w_gate_up[i], w_down[i], w_ple_gate[i], w_ple_proj[i], ln2_g[i], ln2_b[i])
    return x


def setup_inputs(seed: int = 0) -> dict:
    key = jax.random.key(seed)
    ks = jax.random.split(key, 24)
    f32 = jnp.float32

    def nrm(k, shape, scale):
        return jax.random.normal(k, shape, dtype=f32) * scale

    return {
        "x_prompt": nrm(ks[0], (BATCH, SEQ, D_MODEL), 1.0),
        "x_sample": nrm(ks[1], (DEC_BATCH, DEC_SEQ, D_MODEL), 1.0),
        "p_prompt": nrm(ks[2], (DEPTH, BATCH, SEQ, PLE_DIM), 1.0),
        "p_sample": nrm(ks[3], (DEPTH, DEC_BATCH, DEC_SEQ, PLE_DIM), 1.0),
        "w_in": nrm(ks[4], (DEPTH, D_MODEL, W_IN_COLS), D_MODEL ** -0.5),
        "conv_w": nrm(ks[5], (DEPTH, CONV_WIDTH, 1, CONV_CH), CONV_WIDTH ** -0.5),
        "conv_b": nrm(ks[6], (DEPTH, CONV_CH), 0.01),
        "conv_ln_g": 1.0 + nrm(ks[7], (DEPTH, CONV_CH), 0.01),
        "conv_ln_b": nrm(ks[8], (DEPTH, CONV_CH), 0.01),
        "w_conv_out": nrm(ks[9], (DEPTH, CONV_CH, D_MODEL), CONV_CH ** -0.5),
        "w_attn_out": nrm(ks[10], (DEPTH, ATTN_OUT_W, D_MODEL), ATTN_OUT_W ** -0.5),
        "w_out": nrm(ks[11], (DEPTH, D_MODEL, D_MODEL), BETA * D_MODEL ** -0.5),
        "ln1_g": 1.0 + nrm(ks[12], (DEPTH, D_MODEL), 0.01),
        "ln1_b": nrm(ks[13], (DEPTH, D_MODEL), 0.01),
        "w_gate_up": nrm(ks[14], (DEPTH, D_MODEL, 2 * D_FF), D_MODEL ** -0.5),
        "w_down": nrm(ks[15], (DEPTH, D_FF, D_MODEL), BETA * D_FF ** -0.5),
        "w_ple_gate": nrm(ks[16], (DEPTH, D_MODEL, D_MODEL), D_MODEL ** -0.5),
        "w_ple_proj": nrm(ks[17], (DEPTH, PLE_DIM, D_MODEL), BETA * PLE_DIM ** -0.5),
        "ln2_g": 1.0 + nrm(ks[18], (DEPTH, D_MODEL), 0.01),
        "ln2_b": nrm(ks[19], (DEPTH, D_MODEL), 0.01),
    }


def reference(x_prompt, x_sample, p_prompt, p_sample, w_in, conv_w, conv_b, conv_ln_g, conv_ln_b,
              w_conv_out, w_attn_out, w_out, ln1_g, ln1_b, w_gate_up, w_down, w_ple_gate, w_ple_proj,
              ln2_g, ln2_b):
    y_prompt = trunk(x_prompt, p_prompt, w_in, conv_w, conv_b, conv_ln_g, conv_ln_b, w_conv_out,
                     w_attn_out, w_out, ln1_g, ln1_b, w_gate_up, w_down, w_ple_gate, w_ple_proj,
                     ln2_g, ln2_b)
    y_sample = trunk(x_sample, p_sample, w_in, conv_w, conv_b, conv_ln_g, conv_ln_b, w_conv_out,
                     w_attn_out, w_out, ln1_g, ln1_b, w_gate_up, w_down, w_ple_gate, w_ple_proj,
                     ln2_g, ln2_b)
    return (y_prompt, y_sample)
```

```python
import functools
import math

import jax
import jax.numpy as jnp
from jax import lax
from jax.experimental import pallas as pl
from jax.experimental.pallas import tpu as pltpu

D_MODEL = 4096
CONV_CH = 2048
CONV_WIDTH = 31
CONV_HALF = CONV_WIDTH // 2
HEAD_DIM = 128
N_SLOTS = 8
DILATIONS = (1, 4, 16)
RADIUS = 64
N_GROUPS = len(DILATIONS)
GROUP_W = N_SLOTS * HEAD_DIM
ATTN_W = N_GROUPS * GROUP_W
ROPE_DIM = HEAD_DIM // 4
ROPE_HALF = ROPE_DIM // 2
ROPE_THETA = 500000.0
D_FF = 11008
PLE_DIM = 256
ALPHA = 2.0 ** 0.25
LN_EPS = 1e-5
NEG_INF = -1e30

V7X_VMEM_BYTES = 64 * 1024 * 1024
V7X_VMEM_LIMIT_CAP = 56 * 1024 * 1024
LANES = 128

F32 = jnp.float32
BF16 = jnp.bfloat16


def _nbytes(shape, dtype):
    return math.prod(shape) * jnp.dtype(dtype).itemsize


def _vmem_limit(block_bytes, temp_bytes):
    est = 2 * block_bytes + temp_bytes + (4 << 20)
    return int(min(max(est, 16 << 20), V7X_VMEM_LIMIT_CAP))


def _mm_body(*refs, n_lhs, dots, n_extra, epilogue):
    lhs = refs[:n_lhs]
    rhs = refs[n_lhs:n_lhs + len(dots)]
    extra = refs[n_lhs + len(dots):n_lhs + len(dots) + n_extra]
    outs = refs[n_lhs + len(dots) + n_extra:]
    accs = [jnp.dot(lhs[li][...], r[...], preferred_element_type=F32)
            for li, r in zip(dots, rhs)]
    epilogue(pl.program_id(1), accs, extra, outs)


def _matmul(name, lhs, rhs, extras, outs, epilogue, *, tm, tn, n_j):
    m = lhs[0].shape[0]
    grid = (m // tm, n_j)
    in_specs, args, blk = [], [], 0
    for a in lhs:
        in_specs.append(pl.BlockSpec((tm, a.shape[1]), lambda i, j: (i, 0)))
        args.append(a)
        blk += _nbytes((tm, a.shape[1]), a.dtype)
    for _, w, off in rhs:
        in_specs.append(pl.BlockSpec((w.shape[0], tn), lambda i, j, off=off: (0, off + j)))
        args.append(w)
        blk += _nbytes((w.shape[0], tn), w.dtype)
    for a, bs, im in extras:
        in_specs.append(pl.BlockSpec(bs, im))
        args.append(a)
        blk += _nbytes([b for b in bs if b is not None], a.dtype)
    out_shape, out_specs = [], []
    for sds, bs, im in outs:
        out_shape.append(sds)
        out_specs.append(pl.BlockSpec(bs, im))
        blk += _nbytes([b for b in bs if b is not None], sds.dtype)
    temp = (len(rhs) + 3) * tm * tn * 4
    body = functools.partial(_mm_body, n_lhs=len(lhs), dots=tuple(r[0] for r in rhs),
                             n_extra=len(extras), epilogue=epilogue)
    return pl.pallas_call(
        body,
        out_shape=out_shape,
        grid=grid,
        in_specs=in_specs,
        out_specs=out_specs,
        compiler_params=pltpu.CompilerParams(
            dimension_semantics=("parallel", "arbitrary"),
            vmem_limit_bytes=_vmem_limit(blk, temp)),
        name=name,
    )(*args)


def _ep_glu(j, accs, extra, outs):
    a, b = accs
    outs[0][...] = a * jax.nn.sigmoid(b)


def _ep_sigmoid(j, accs, extra, outs):
    outs[0][...] = jax.nn.sigmoid(accs[0]).astype(outs[0].dtype)


def _ep_qkv(j, accs, extra, outs, *, n_rope_blocks, heads_per_block):
    acc = accs[0]
    cos_ref, sin_ref = extra
    o_ref = outs[0]

    @pl.when(j < n_rope_blocks)
    def _():
        cos = jnp.concatenate([cos_ref[...]] * heads_per_block, axis=1)
        sin = jnp.concatenate([sin_ref[...]] * heads_per_block, axis=1)
        width = acc.shape[1]
        lane = lax.broadcasted_iota(jnp.int32, acc.shape, 1) % HEAD_DIM
        partner = jnp.where(lane < ROPE_HALF,
                            pltpu.roll(acc, width - ROPE_HALF, axis=1),
                            pltpu.roll(acc, ROPE_HALF, axis=1))
        o_ref[...] = (acc * cos + partner * sin).astype(o_ref.dtype)

    @pl.when(j >= n_rope_blocks)
    def _():
        o_ref[...] = acc.astype(o_ref.dtype)


def _ep_merge(j, accs, extra, outs):
    conv, attn = accs
    gc_ref, ga_ref = extra
    merged = gc_ref[...].astype(F32) * conv + ga_ref[...].astype(F32) * attn
    outs[0][...] = merged.astype(outs[0].dtype)


def _ep_residual(j, accs, extra, outs, *, scale):
    outs[0][...] = scale * extra[0][...] + accs[0]


def _ep_swiglu(j, accs, extra, outs):
    g, u = accs
    outs[0][...] = (g * jax.nn.sigmoid(g) * u).astype(outs[0].dtype)


def _ep_ple(j, accs, extra, outs):
    gate, proj = accs
    outs[0][...] = ALPHA * extra[0][...] + jax.nn.sigmoid(gate) * proj


CONV_TS = 256
CONV_HALO = 16
CONV_CW = 128


def _conv_body(prev_ref, cur_ref, next_ref, w_ref, b_ref, g_ref, beta_ref, o_ref, xp_ref, y_ref):
    lb = pl.program_id(1)
    n_lb = pl.num_programs(1)
    ts = cur_ref.shape[0]
    prev = jnp.where(lb > 0, prev_ref[...], 0.0)
    nxt = jnp.where(lb < n_lb - 1, next_ref[...], 0.0)
    xp_ref[0:CONV_HALO, :] = prev
    xp_ref[CONV_HALO:CONV_HALO + ts, :] = cur_ref[...]
    xp_ref[CONV_HALO + ts:, :] = nxt
    base = CONV_HALO - CONV_HALF
    for c in range(CONV_CH // CONV_CW):
        cs = slice(c * CONV_CW, (c + 1) * CONV_CW)
        acc = jnp.zeros((ts, CONV_CW), F32) + b_ref[:, cs]
        for tap in range(CONV_WIDTH):
            acc = acc + xp_ref[base + tap:base + tap + ts, cs] * w_ref[tap:tap + 1, cs]
        y_ref[:, cs] = acc
    y = y_ref[...]
    mu = jnp.mean(y, axis=-1, keepdims=True)
    yc = y - mu
    var = jnp.mean(yc * yc, axis=-1, keepdims=True)
    z = yc * lax.rsqrt(var + LN_EPS) * g_ref[...] + beta_ref[...]
    o_ref[...] = (z * jax.nn.sigmoid(z)).astype(o_ref.dtype)


def _conv_module(u, conv_w, conv_b, ln_g, ln_b, n_seq, seq_len):
    t = u.shape[0]
    ts = CONV_TS
    n_lb = seq_len // ts
    r = ts // CONV_HALO
    n_halo_blocks = t // CONV_HALO

    def cur_map(b, lb):
        return (b * n_lb + lb, 0)

    def prev_map(b, lb):
        return (jnp.maximum((b * n_lb + lb) * r - 1, 0), 0)

    def next_map(b, lb):
        return (jnp.minimum((b * n_lb + lb + 1) * r, n_halo_blocks - 1), 0)

    vec = pl.BlockSpec((1, CONV_CH), lambda b, lb: (0, 0))
    blk = _nbytes((ts + 2 * CONV_HALO, CONV_CH), F32) + _nbytes((ts, CONV_CH), BF16)
    return pl.pallas_call(
        _conv_body,
        out_shape=jax.ShapeDtypeStruct((t, CONV_CH), BF16),
        grid=(n_seq, n_lb),
        in_specs=[
            pl.BlockSpec((CONV_HALO, CONV_CH), prev_map),
            pl.BlockSpec((ts, CONV_CH), cur_map),
            pl.BlockSpec((CONV_HALO, CONV_CH), next_map),
            pl.BlockSpec((CONV_WIDTH, CONV_CH), lambda b, lb: (0, 0)),
            vec, vec, vec,
        ],
        out_specs=pl.BlockSpec((ts, CONV_CH), cur_map),
        scratch_shapes=[pltpu.VMEM((ts + 2 * CONV_HALO, CONV_CH), F32),
                        pltpu.VMEM((ts, CONV_CH), F32)],
        compiler_params=pltpu.CompilerParams(
            dimension_semantics=("parallel", "arbitrary"),
            vmem_limit_bytes=_vmem_limit(blk, 3 * _nbytes((ts + 2 * CONV_HALO, CONV_CH), F32))),
        name="conv_module",
    )(u, u, u, conv_w, conv_b, ln_g, ln_b)


ATT_BQ = 128


def _attn_body(q_ref, kp_ref, kc_ref, kn_ref, vp_ref, vc_ref, vn_ref, o_ref, lse_ref, k_buf, v_buf):
    lb = pl.program_id(1)
    n_lb = pl.num_programs(1)
    bq = q_ref.shape[0]
    nk = bq + 2 * RADIUS
    k_buf[0:RADIUS, :] = kp_ref[...]
    k_buf[RADIUS:RADIUS + bq, :] = kc_ref[...]
    k_buf[RADIUS + bq:, :] = kn_ref[...]
    v_buf[0:RADIUS, :] = vp_ref[...]
    v_buf[RADIUS:RADIUS + bq, :] = vc_ref[...]
    v_buf[RADIUS + bq:, :] = vn_ref[...]

    qi = lax.broadcasted_iota(jnp.int32, (bq, nk), 0)
    kt = lax.broadcasted_iota(jnp.int32, (bq, nk), 1)
    valid = (kt >= qi) & (kt <= qi + 2 * RADIUS)
    valid &= (kt >= RADIUS) | (lb > 0)
    valid &= (kt < RADIUS + bq) | (lb < n_lb - 1)
    scale = 1.0 / math.sqrt(HEAD_DIM)

    for h in range(N_SLOTS):
        hs = slice(h * HEAD_DIM, (h + 1) * HEAD_DIM)
        s = lax.dot_general(q_ref[:, hs], k_buf[:, hs], (((1,), (1,)), ((), ())),
                            preferred_element_type=F32)
        s = jnp.where(valid, s * scale, NEG_INF)
        m = jnp.max(s, axis=-1, keepdims=True)
        e = jnp.exp(s - m)
        den = jnp.sum(e, axis=-1, keepdims=True)
        pv = jnp.dot(e.astype(BF16), v_buf[:, hs], preferred_element_type=F32)
        o_ref[:, hs] = pv / den
        lse_ref[:, hs] = jnp.broadcast_to(m + jnp.log(den), (bq, HEAD_DIM))


def _attention_group(qkv, g, dilation, n_seq, seq_len):
    t = qkv.shape[1]
    d = dilation
    sub_len = seq_len // d
    bq = min(ATT_BQ, sub_len)
    n_lb = sub_len // bq
    rows = t // d
    x = qkv.reshape(3 * N_GROUPS, rows, d * GROUP_W)
    hr = bq // RADIUS
    n_halo = rows // RADIUS

    def q_map(which):
        return lambda b, lb, r: (which * N_GROUPS + g, b * n_lb + lb, r)

    def prev_map(which):
        return lambda b, lb, r: (which * N_GROUPS + g,
                                 jnp.maximum((b * n_lb + lb) * hr - 1, 0), r)

    def next_map(which):
        return lambda b, lb, r: (which * N_GROUPS + g,
                                 jnp.minimum((b * n_lb + lb + 1) * hr, n_halo - 1), r)

    cur = lambda which: pl.BlockSpec((None, bq, GROUP_W), q_map(which))
    halo_p = lambda which: pl.BlockSpec((None, RADIUS, GROUP_W), prev_map(which))
    halo_n = lambda which: pl.BlockSpec((None, RADIUS, GROUP_W), next_map(which))
    out_spec = pl.BlockSpec((bq, GROUP_W), lambda b, lb, r: (b * n_lb + lb, r))
    nk = bq + 2 * RADIUS
    blk = 3 * _nbytes((nk, GROUP_W), BF16) + 2 * _nbytes((bq, GROUP_W), F32)
    o, lse = pl.pallas_call(
        _attn_body,
        out_shape=[jax.ShapeDtypeStruct((rows, d * GROUP_W), F32)] * 2,
        grid=(n_seq, n_lb, d),
        in_specs=[cur(0), halo_p(1), cur(1), halo_n(1), halo_p(2), cur(2), halo_n(2)],
        out_specs=[out_spec, out_spec],
        scratch_shapes=[pltpu.VMEM((nk, GROUP_W), BF16), pltpu.VMEM((nk, GROUP_W), BF16)],
        compiler_params=pltpu.CompilerParams(
            dimension_semantics=("parallel", "arbitrary", "arbitrary"),
            vmem_limit_bytes=_vmem_limit(blk, 8 << 20)),
        name=f"band_attention_d{d}",
    )(x, x, x, x, x, x, x)
    return o.reshape(t, GROUP_W), lse.reshape(t, GROUP_W)


def _combine_body(o0, o1, o2, l0, l1, l2, out_ref):
    a, b, c = l0[...], l1[...], l2[...]
    m = jnp.maximum(jnp.maximum(a, b), c)
    ea, eb, ec = jnp.exp(a - m), jnp.exp(b - m), jnp.exp(c - m)
    num = ea * o0[...] + eb * o1[...] + ec * o2[...]
    out_ref[...] = (num / (ea + eb + ec)).astype(out_ref.dtype)


def _combine_groups(os_, lses):
    t = os_[0].shape[0]
    tr = 256
    spec = pl.BlockSpec((tr, GROUP_W), lambda i: (i, 0))
    return pl.pallas_call(
        _combine_body,
        out_shape=jax.ShapeDtypeStruct((t, GROUP_W), BF16),
        grid=(t // tr,),
        in_specs=[spec] * 6,
        out_specs=spec,
        compiler_params=pltpu.CompilerParams(
            dimension_semantics=("parallel",),
            vmem_limit_bytes=_vmem_limit(7 * _nbytes((tr, GROUP_W), F32), 8 << 20)),
        name="combine_groups",
    )(*os_, *lses)


def _ln_body(x_ref, g_ref, b_ref, *out_refs):
    x = x_ref[...]
    mu = jnp.mean(x, axis=-1, keepdims=True)
    xc = x - mu
    var = jnp.mean(xc * xc, axis=-1, keepdims=True)
    y = xc * lax.rsqrt(var + LN_EPS) * g_ref[...] + b_ref[...]
    for o in out_refs:
        o[...] = y.astype(o.dtype)


def _layer_norm(name, x, g, b, out_dtypes):
    t, dm = x.shape
    tr = 256
    spec = pl.BlockSpec((tr, dm), lambda i: (i, 0))
    vec = pl.BlockSpec((1, dm), lambda i: (0, 0))
    return pl.pallas_call(
        _ln_body,
        out_shape=[jax.ShapeDtypeStruct((t, dm), dt) for dt in out_dtypes],
        grid=(t // tr,),
        in_specs=[spec, vec, vec],
        out_specs=[spec] * len(out_dtypes),
        compiler_params=pltpu.CompilerParams(
            dimension_semantics=("parallel",),
            vmem_limit_bytes=_vmem_limit((1 + len(out_dtypes)) * _nbytes((tr, dm), F32),
                                         4 * _nbytes((tr, dm), F32))),
        name=name,
    )(x, g, b)


def _rope_tables(seq_len):
    pos = jnp.arange(seq_len, dtype=F32)
    inv = ROPE_THETA ** (-jnp.arange(0, ROPE_DIM, 2, dtype=F32) / ROPE_DIM)
    ang = pos[:, None] * inv[None, :]
    cos, sin = jnp.cos(ang), jnp.sin(ang)
    pad1 = jnp.ones((seq_len, HEAD_DIM - ROPE_DIM), F32)
    pad0 = jnp.zeros((seq_len, HEAD_DIM - ROPE_DIM), F32)
    return (jnp.concatenate([cos, cos, pad1], axis=1),
            jnp.concatenate([-sin, sin, pad0], axis=1))


def _encoder_layer(x, p, wts, n_seq, seq_len):
    t = x.shape[0]
    tm = 1024
    tn = 512
    xb = x.astype(BF16)
    row_tile = lambda width: ((tm, width), lambda i, j: (i, j))

    w_in = wts["w_in"]
    o_q = 2 * CONV_CH
    o_gate = o_q + 3 * ATTN_W
    (u,) = _matmul(
        "proj_glu", [xb], [(0, w_in, 0), (0, w_in, CONV_CH // tn)], [],
        [(jax.ShapeDtypeStruct((t, CONV_CH), F32), *row_tile(tn))],
        _ep_glu, tm=tm, tn=tn, n_j=CONV_CH // tn)

    cos_t, sin_t = _rope_tables(seq_len)
    pos_blocks = seq_len // tm
    tab = ((tm, HEAD_DIM), lambda i, j: (i % pos_blocks, 0))
    per_group = GROUP_W // tn
    (qkv,) = _matmul(
        "proj_qkv", [xb], [(0, w_in, o_q // tn)], [(cos_t, *tab), (sin_t, *tab)],
        [(jax.ShapeDtypeStruct((3 * N_GROUPS, t, GROUP_W), BF16), (None, tm, tn),
          lambda i, j: (j // per_group, i, j % per_group))],
        functools.partial(_ep_qkv, n_rope_blocks=2 * ATTN_W // tn, heads_per_block=tn // HEAD_DIM),
        tm=tm, tn=tn, n_j=3 * ATTN_W // tn)

    (gates,) = _matmul(
        "proj_gates", [xb], [(0, w_in, o_gate // tn)], [],
        [(jax.ShapeDtypeStruct((t, 2 * D_MODEL), BF16), *row_tile(tn))],
        _ep_sigmoid, tm=tm, tn=tn, n_j=2 * D_MODEL // tn)

    u2 = _conv_module(u, wts["conv_w"], wts["conv_b"], wts["conv_ln_g"], wts["conv_ln_b"],
                      n_seq, seq_len)

    os_, lses = [], []
    for g, d in enumerate(DILATIONS):
        o_g, lse_g = _attention_group(qkv, g, d, n_seq, seq_len)
        os_.append(o_g)
        lses.append(lse_g)
    attn = _combine_groups(os_, lses)

    n_dj = D_MODEL // tn
    (merged,) = _matmul(
        "merge", [u2, attn], [(0, wts["w_conv_out"], 0), (1, wts["w_attn_out"], 0)],
        [(gates, *row_tile(tn)), (gates, (tm, tn), lambda i, j: (i, n_dj + j))],
        [(jax.ShapeDtypeStruct((t, D_MODEL), BF16), *row_tile(tn))],
        _ep_merge, tm=tm, tn=tn, n_j=n_dj)
    (pre1,) = _matmul(
        "out_proj", [merged], [(0, wts["w_out"], 0)], [(x, *row_tile(tn))],
        [(jax.ShapeDtypeStruct((t, D_MODEL), F32), *row_tile(tn))],
        functools.partial(_ep_residual, scale=ALPHA), tm=tm, tn=tn, n_j=n_dj)
    x1, x1b = _layer_norm("ln1", pre1, wts["ln1_g"], wts["ln1_b"], (F32, BF16))

    tf = 256
    (act,) = _matmul(
        "ffn_up", [x1b], [(0, wts["w_gate_up"], 0), (0, wts["w_gate_up"], D_FF // tf)], [],
        [(jax.ShapeDtypeStruct((t, D_FF), BF16), *row_tile(tf))],
        _ep_swiglu, tm=tm, tn=tf, n_j=D_FF // tf)
    pb = p.astype(BF16)
    (res2,) = _matmul(
        "ple", [x1b, pb], [(0, wts["w_ple_gate"], 0), (1, wts["w_ple_proj"], 0)],
        [(x1, *row_tile(tn))],
        [(jax.ShapeDtypeStruct((t, D_MODEL), F32), *row_tile(tn))],
        _ep_ple, tm=tm, tn=tn, n_j=n_dj)
    td_m, td_n = 512, 256
    (pre2,) = _matmul(
        "ffn_down", [act], [(0, wts["w_down"], 0)],
        [(res2, (td_m, td_n), lambda i, j: (i, j))],
        [(jax.ShapeDtypeStruct((t, D_MODEL), F32), (td_m, td_n), lambda i, j: (i, j))],
        functools.partial(_ep_residual, scale=1.0), tm=td_m, tn=td_n, n_j=D_MODEL // td_n)
    (y,) = _layer_norm("ln2", pre2, wts["ln2_g"], wts["ln2_b"], (F32,))
    return y


def kernel(x_prompt, x_sample, p_prompt, p_sample, w_in, conv_w, conv_b, conv_ln_g, conv_ln_b,
           w_conv_out, w_attn_out, w_out, ln1_g, ln1_b, w_gate_up, w_down, w_ple_gate, w_ple_proj,
           ln2_g, ln2_b):
    depth = w_in.shape[0]
    layers = []
    for i in range(depth):
        layers.append({
            "w_in": w_in[i].astype(BF16),
            "conv_w": conv_w[i].reshape(CONV_WIDTH, CONV_CH),
            "conv_b": conv_b[i].reshape(1, CONV_CH),
            "conv_ln_g": conv_ln_g[i].reshape(1, CONV_CH),
            "conv_ln_b": conv_ln_b[i].reshape(1, CONV_CH),
            "w_conv_out": w_conv_out[i].astype(BF16),
            "w_attn_out": w_attn_out[i].astype(BF16),
            "w_out": w_out[i].astype(BF16),
            "ln1_g": ln1_g[i].reshape(1, D_MODEL),
            "ln1_b": ln1_b[i].reshape(1, D_MODEL),
            "w_gate_up": w_gate_up[i].astype(BF16),
            "w_down": w_down[i].astype(BF16),
            "w_ple_gate": w_ple_gate[i].astype(BF16),
            "w_ple_proj": w_ple_proj[i].astype(BF16),
            "ln2_g": ln2_g[i].reshape(1, D_MODEL),
            "ln2_b": ln2_b[i].reshape(1, D_MODEL),
        })
    outs = []
    for x, p in ((x_prompt, p_prompt), (x_sample, p_sample)):
        n_seq, seq_len, _ = x.shape
        h = x.reshape(n_seq * seq_len, D_MODEL)
        for i, wts in enumerate(layers):
            h = _encoder_layer(h, p[i].reshape(n_seq * seq_len, PLE_DIM), wts, n_seq, seq_len)
        outs.append(h.reshape(n_seq, seq_len, D_MODEL))
    return tuple(outs)
```

```python
import functools
import math

import jax
import jax.numpy as jnp
from jax import lax
from jax.experimental import pallas as pl
from jax.experimental.pallas import tpu as pltpu

D_MODEL = 4096
CONV_CH = 2048
CONV_WIDTH = 31
CONV_HALF = CONV_WIDTH // 2
HEAD_DIM = 128
N_SLOTS = 8
DILATIONS = (1, 4, 16)
RADIUS = 64
N_GROUPS = len(DILATIONS)
GROUP_W = N_SLOTS * HEAD_DIM
ATTN_W = N_GROUPS * GROUP_W
ROPE_DIM = HEAD_DIM // 4
ROPE_HALF = ROPE_DIM // 2
ROPE_THETA = 500000.0
D_FF = 11008
PLE_DIM = 256
ALPHA = 2.0 ** 0.25
LN_EPS = 1e-5
NEG_INF = -1e30

V7X_VMEM_LIMIT_CAP = 56 * 1024 * 1024
M_SPLIT = 4
LANES = 128
SUBLANES = 8

F32 = jnp.float32
BF16 = jnp.bfloat16


def _nbytes(shape, dtype):
    return math.prod(shape) * jnp.dtype(dtype).itemsize


def _vmem_limit(block_bytes, temp_bytes):
    est = 2 * block_bytes + temp_bytes + (4 << 20)
    return int(min(max(est, 16 << 20), V7X_VMEM_LIMIT_CAP))


def _mm_body(*refs, n_lhs, dots, n_extra, n_out, m_split, epilogue):
    lhs = refs[:n_lhs]
    rhs = refs[n_lhs:n_lhs + len(dots)]
    extra = refs[n_lhs + len(dots):n_lhs + len(dots) + n_extra]
    outs = refs[n_lhs + len(dots) + n_extra:n_lhs + len(dots) + n_extra + n_out]
    scratch = refs[n_lhs + len(dots) + n_extra + n_out:]
    rows = lhs[0].shape[0] // m_split
    for c in range(m_split):
        rs = slice(c * rows, (c + 1) * rows)
        accs = [jnp.dot(lhs[li][rs, :], r[...], preferred_element_type=F32)
                for li, r in zip(dots, rhs)]
        epilogue(rs, accs, extra, outs, scratch)


def _matmul(name, lhs, rhs, extras, outs, epilogue, *, tm, tn, n_j, m_split=M_SPLIT, scratch=()):
    m = lhs[0].shape[0]
    grid = (m // tm, n_j)
    in_specs, args, blk = [], [], 0
    for a in lhs:
        in_specs.append(pl.BlockSpec((tm, a.shape[1]), lambda i, j: (i, 0)))
        args.append(a)
        blk += _nbytes((tm, a.shape[1]), a.dtype)
    for _, w, col in rhs:
        in_specs.append(pl.BlockSpec((w.shape[0], tn), lambda i, j, col=col: (0, col(j))))
        args.append(w)
        blk += _nbytes((w.shape[0], tn), w.dtype)
    for a, bs, im in extras:
        in_specs.append(pl.BlockSpec(bs, im))
        args.append(a)
        blk += _nbytes([b for b in bs if b is not None], a.dtype)
    out_shape, out_specs = [], []
    for sds, bs, im in outs:
        out_shape.append(sds)
        out_specs.append(pl.BlockSpec(bs, im))
        blk += _nbytes([b for b in bs if b is not None], sds.dtype)
    temp = (len(rhs) + 3) * tm * tn * 4 + sum(_nbytes(s.shape, s.dtype) for s in scratch)
    body = functools.partial(_mm_body, n_lhs=len(lhs), dots=tuple(r[0] for r in rhs),
                             n_extra=len(extras), n_out=len(outs), m_split=m_split, epilogue=epilogue)
    return pl.pallas_call(
        body,
        out_shape=out_shape,
        grid=grid,
        in_specs=in_specs,
        out_specs=out_specs,
        scratch_shapes=list(scratch),
        compiler_params=pltpu.CompilerParams(
            dimension_semantics=("parallel", "arbitrary"),
            vmem_limit_bytes=_vmem_limit(blk, temp)),
        name=name,
    )(*args)


def _cols(offset):
    return lambda j: offset + j


def _ep_glu(rs, accs, extra, outs, scratch):
    a, b = accs
    outs[0][rs, :] = a * jax.nn.sigmoid(b)


def _ep_sigmoid(rs, accs, extra, outs, scratch):
    outs[0][rs, :] = jax.nn.sigmoid(accs[0]).astype(outs[0].dtype)


def _ep_qkv(rs, accs, extra, outs, scratch, *, heads_per_block, dilation):
    acc = accs[0]
    cos_ref, sin_ref = extra
    o_ref = outs[0]
    cos = jnp.concatenate([cos_ref[rs, :]] * heads_per_block, axis=1)
    sin = jnp.concatenate([sin_ref[rs, :]] * heads_per_block, axis=1)
    width = acc.shape[1]
    lane = lax.broadcasted_iota(jnp.int32, acc.shape, 1) % HEAD_DIM
    partner = jnp.where(lane < ROPE_HALF,
                        pltpu.roll(acc, width - ROPE_HALF, axis=1),
                        pltpu.roll(acc, ROPE_HALF, axis=1))
    val = acc * cos + partner * sin
    ro = slice(rs.start // dilation, rs.stop // dilation)
    if dilation == 1:
        o_ref[0, ro, :] = val.astype(o_ref.dtype)
    else:
        stage = scratch[0]
        n = val.shape[0] // dilation
        for c in range(width // LANES):
            cs = slice(c * LANES, (c + 1) * LANES)
            stage[c] = val[:, cs]
            for r in range(dilation):
                o_ref[r, ro, cs] = stage[c, pl.ds(r, n, stride=dilation), :].astype(o_ref.dtype)


def _ep_merge(rs, accs, extra, outs, scratch):
    conv, attn = accs
    gc_ref, ga_ref = extra
    merged = gc_ref[rs, :].astype(F32) * conv + ga_ref[rs, :].astype(F32) * attn
    outs[0][rs, :] = merged.astype(outs[0].dtype)


def _ep_residual(rs, accs, extra, outs, scratch, *, scale):
    outs[0][rs, :] = scale * extra[0][rs, :] + accs[0]


def _ep_swiglu(rs, accs, extra, outs, scratch):
    g, u = accs
    outs[0][rs, :] = (g * jax.nn.sigmoid(g) * u).astype(outs[0].dtype)


def _ep_ple(rs, accs, extra, outs, scratch):
    gate, proj = accs
    outs[0][rs, :] = ALPHA * extra[0][rs, :] + jax.nn.sigmoid(gate) * proj


CONV_TS = 256
CONV_HALO = 16
CONV_CW = 128


def _conv_body(prev_ref, cur_ref, next_ref, w_ref, b_ref, g_ref, beta_ref, o_ref, xp_ref, y_ref, sh_ref):
    lb = pl.program_id(1)
    n_lb = pl.num_programs(1)
    ts = cur_ref.shape[0]
    prev = jnp.where(lb > 0, prev_ref[...], 0.0)
    nxt = jnp.where(lb < n_lb - 1, next_ref[...], 0.0)
    xp_ref[0:CONV_HALO, :] = prev
    xp_ref[CONV_HALO:CONV_HALO + ts, :] = cur_ref[...]
    xp_ref[CONV_HALO + ts:, :] = nxt
    base = CONV_HALO - CONV_HALF
    span = ts + (base + CONV_WIDTH - 1) // SUBLANES * SUBLANES
    for c in range(CONV_CH // CONV_CW):
        cs = slice(c * CONV_CW, (c + 1) * CONV_CW)
        for k in range(SUBLANES):
            sh_ref[k] = xp_ref[k:k + span, cs]
        w_c = w_ref[:, cs]
        acc = jnp.broadcast_to(b_ref[:, cs], (ts // SUBLANES, SUBLANES, CONV_CW))
        for tap in range(CONV_WIDTH):
            off = base + tap
            lo = off // SUBLANES * SUBLANES
            wb = jnp.broadcast_to(w_c[tap:tap + 1, :], (SUBLANES, CONV_CW))
            win = sh_ref[off % SUBLANES, lo:lo + ts, :].reshape(ts // SUBLANES, SUBLANES, CONV_CW)
            acc = acc + win * wb[None]
        y_ref[:, cs] = acc.reshape(ts, CONV_CW)
    y = y_ref[...]
    mu = jnp.mean(y, axis=-1, keepdims=True)
    yc = y - mu
    var = jnp.mean(yc * yc, axis=-1, keepdims=True)
    z = yc * lax.rsqrt(var + LN_EPS) * g_ref[...] + beta_ref[...]
    o_ref[...] = (z * jax.nn.sigmoid(z)).astype(o_ref.dtype)


def _conv_module(u, conv_w, conv_b, ln_g, ln_b, n_seq, seq_len):
    t = u.shape[0]
    ts = CONV_TS
    n_lb = seq_len // ts
    r = ts // CONV_HALO
    n_halo_blocks = t // CONV_HALO

    def cur_map(b, lb):
        return (b * n_lb + lb, 0)

    def prev_map(b, lb):
        return (jnp.maximum((b * n_lb + lb) * r - 1, 0), 0)

    def next_map(b, lb):
        return (jnp.minimum((b * n_lb + lb + 1) * r, n_halo_blocks - 1), 0)

    vec = pl.BlockSpec((1, CONV_CH), lambda b, lb: (0, 0))
    blk = _nbytes((ts + 2 * CONV_HALO, CONV_CH), F32) + _nbytes((ts, CONV_CH), BF16)
    return pl.pallas_call(
        _conv_body,
        out_shape=jax.ShapeDtypeStruct((t, CONV_CH), BF16),
        grid=(n_seq, n_lb),
        in_specs=[
            pl.BlockSpec((CONV_HALO, CONV_CH), prev_map),
            pl.BlockSpec((ts, CONV_CH), cur_map),
            pl.BlockSpec((CONV_HALO, CONV_CH), next_map),
            pl.BlockSpec((CONV_WIDTH, CONV_CH), lambda b, lb: (0, 0)),
            vec, vec, vec,
        ],
        out_specs=pl.BlockSpec((ts, CONV_CH), cur_map),
        scratch_shapes=[pltpu.VMEM((ts + 2 * CONV_HALO, CONV_CH), F32),
                        pltpu.VMEM((ts, CONV_CH), F32),
                        pltpu.VMEM((SUBLANES, ts + 2 * CONV_HALO - SUBLANES, CONV_CW), F32)],
        compiler_params=pltpu.CompilerParams(
            dimension_semantics=("parallel", "arbitrary"),
            vmem_limit_bytes=_vmem_limit(blk, 10 * _nbytes((ts + 2 * CONV_HALO, CONV_CH), F32))),
        name="conv_module",
    )(u, u, u, conv_w, conv_b, ln_g, ln_b)


ATT_BQ = 512
ATT_SUB = 128


def _attn_body(q_ref, kp_ref, kc_ref, kn_ref, vp_ref, vc_ref, vn_ref, o_ref, lse_ref, k_buf, v_buf):
    lb = pl.program_id(1)
    n_lb = pl.num_programs(1)
    bq = q_ref.shape[0]
    sub = min(ATT_SUB, bq)
    n_sub = bq // sub
    nk = sub + 2 * RADIUS
    k_buf[0:RADIUS, :] = kp_ref[...]
    k_buf[RADIUS:RADIUS + bq, :] = kc_ref[...]
    k_buf[RADIUS + bq:, :] = kn_ref[...]
    v_buf[0:RADIUS, :] = vp_ref[...]
    v_buf[RADIUS:RADIUS + bq, :] = vc_ref[...]
    v_buf[RADIUS + bq:, :] = vn_ref[...]

    qi = lax.broadcasted_iota(jnp.int32, (sub, nk), 0)
    kt = lax.broadcasted_iota(jnp.int32, (sub, nk), 1)
    band = (kt >= qi) & (kt <= qi + 2 * RADIUS)
    first = band & ((kt >= RADIUS) | (lb > 0))
    last = band & ((kt < RADIUS + sub) | (lb < n_lb - 1))
    scale = 1.0 / math.sqrt(HEAD_DIM)

    for sb in range(n_sub):
        valid = band
        if sb == 0:
            valid = first
        if sb == n_sub - 1:
            valid = valid & last
        qs = slice(sb * sub, (sb + 1) * sub)
        ks = slice(sb * sub, sb * sub + nk)
        for h in range(N_SLOTS):
            hs = slice(h * HEAD_DIM, (h + 1) * HEAD_DIM)
            s = lax.dot_general(q_ref[qs, hs], k_buf[ks, hs], (((1,), (1,)), ((), ())),
                                preferred_element_type=F32)
            s = jnp.where(valid, s * scale, NEG_INF)
            m = jnp.max(s, axis=-1, keepdims=True)
            e = jnp.exp(s - m)
            den = jnp.sum(e, axis=-1, keepdims=True)
            pv = jnp.dot(e.astype(BF16), v_buf[ks, hs], preferred_element_type=F32)
            o_ref[qs, hs] = pv / den
            lse_ref[qs, hs] = jnp.broadcast_to(m + jnp.log(den), (sub, HEAD_DIM))


def _attention_group(qkv, dilation, n_seq, seq_len):
    _, d, rows, _ = qkv.shape
    sub_len = seq_len // d
    bq = min(ATT_BQ, sub_len)
    n_lb = sub_len // bq
    hr = bq // RADIUS
    n_halo = rows // RADIUS

    def cur(which):
        return pl.BlockSpec((None, None, bq, GROUP_W),
                            lambda b, lb, r: (which, r, b * n_lb + lb, 0))

    def halo_prev(which):
        return pl.BlockSpec((None, None, RADIUS, GROUP_W),
                            lambda b, lb, r: (which, r, jnp.maximum((b * n_lb + lb) * hr - 1, 0), 0))

    def halo_next(which):
        return pl.BlockSpec((None, None, RADIUS, GROUP_W),
                            lambda b, lb, r: (which, r, jnp.minimum((b * n_lb + lb + 1) * hr, n_halo - 1), 0))

    out_spec = pl.BlockSpec((None, bq, GROUP_W), lambda b, lb, r: (r, b * n_lb + lb, 0))
    nk = bq + 2 * RADIUS
    blk = 3 * _nbytes((nk, GROUP_W), BF16) + 2 * _nbytes((bq, GROUP_W), F32)
    return pl.pallas_call(
        _attn_body,
        out_shape=[jax.ShapeDtypeStruct((d, rows, GROUP_W), F32)] * 2,
        grid=(n_seq, n_lb, d),
        in_specs=[cur(0), halo_prev(1), cur(1), halo_next(1), halo_prev(2), cur(2), halo_next(2)],
        out_specs=[out_spec, out_spec],
        scratch_shapes=[pltpu.VMEM((nk, GROUP_W), BF16), pltpu.VMEM((nk, GROUP_W), BF16)],
        compiler_params=pltpu.CompilerParams(
            dimension_semantics=("parallel", "arbitrary", "arbitrary"),
            vmem_limit_bytes=_vmem_limit(blk, 2 * _nbytes((nk, GROUP_W), BF16) + (8 << 20))),
        name=f"band_attention_d{d}",
    )(qkv, qkv, qkv, qkv, qkv, qkv, qkv)


COMBINE_ROWS = 512


def _combine_body(*refs):
    o_refs = refs[0:N_GROUPS]
    l_refs = refs[N_GROUPS:2 * N_GROUPS]
    out_ref = refs[2 * N_GROUPS]
    scratch = list(refs[2 * N_GROUPS + 1:])

    def position_major(ref):
        d, n, w = ref.shape
        if d == 1:
            return ref[0]
        buf = scratch.pop()
        for c in range(w // LANES):
            for r in range(d):
                buf[c, pl.ds(r, n, stride=d), :] = ref[r, :, c * LANES:(c + 1) * LANES]
        return jnp.concatenate([buf[c] for c in range(w // LANES)], axis=1)

    os_ = [position_major(r) for r in o_refs]
    ls = [position_major(r) for r in l_refs]
    m = functools.reduce(jnp.maximum, ls)
    es = [jnp.exp(l - m) for l in ls]
    num = sum(e * o for e, o in zip(es, os_))
    out_ref[...] = (num / sum(es)).astype(out_ref.dtype)


def _combine_groups(os_, lses):
    t = os_[0].shape[0] * os_[0].shape[1]
    tr = COMBINE_ROWS
    specs = [pl.BlockSpec((a.shape[0], tr // a.shape[0], GROUP_W), lambda i: (0, i, 0))
             for a in (*os_, *lses)]
    n_scratch = sum(1 for a in (*os_, *lses) if a.shape[0] > 1)
    return pl.pallas_call(
        _combine_body,
        out_shape=jax.ShapeDtypeStruct((t, GROUP_W), BF16),
        grid=(t // tr,),
        in_specs=specs,
        out_specs=pl.BlockSpec((tr, GROUP_W), lambda i: (i, 0)),
        scratch_shapes=[pltpu.VMEM((GROUP_W // LANES, tr, LANES), F32)] * n_scratch,
        compiler_params=pltpu.CompilerParams(
            dimension_semantics=("parallel",),
            vmem_limit_bytes=_vmem_limit(7 * _nbytes((tr, GROUP_W), F32),
                                         (n_scratch + 6) * _nbytes((tr, GROUP_W), F32))),
        name="combine_groups",
    )(*os_, *lses)


def _ln_body(x_ref, g_ref, b_ref, *out_refs):
    x = x_ref[...]
    mu = jnp.mean(x, axis=-1, keepdims=True)
    xc = x - mu
    var = jnp.mean(xc * xc, axis=-1, keepdims=True)
    y = xc * lax.rsqrt(var + LN_EPS) * g_ref[...] + b_ref[...]
    for o in out_refs:
        o[...] = y.astype(o.dtype)


def _layer_norm(name, x, g, b, out_dtypes):
    t, dm = x.shape
    tr = 256
    spec = pl.BlockSpec((tr, dm), lambda i: (i, 0))
    vec = pl.BlockSpec((1, dm), lambda i: (0, 0))
    return pl.pallas_call(
        _ln_body,
        out_shape=[jax.ShapeDtypeStruct((t, dm), dt) for dt in out_dtypes],
        grid=(t // tr,),
        in_specs=[spec, vec, vec],
        out_specs=[spec] * len(out_dtypes),
        compiler_params=pltpu.CompilerParams(
            dimension_semantics=("parallel",),
            vmem_limit_bytes=_vmem_limit((1 + len(out_dtypes)) * _nbytes((tr, dm), F32),
                                         4 * _nbytes((tr, dm), F32))),
        name=name,
    )(x, g, b)


def _rope_tables(seq_len):
    pos = jnp.arange(seq_len, dtype=F32)
    inv = ROPE_THETA ** (-jnp.arange(0, ROPE_DIM, 2, dtype=F32) / ROPE_DIM)
    ang = pos[:, None] * inv[None, :]
    cos, sin = jnp.cos(ang), jnp.sin(ang)
    pad1 = jnp.ones((seq_len, HEAD_DIM - ROPE_DIM), F32)
    pad0 = jnp.zeros((seq_len, HEAD_DIM - ROPE_DIM), F32)
    cos_t = jnp.concatenate([cos, cos, pad1], axis=1)
    sin_t = jnp.concatenate([-sin, sin, pad0], axis=1)
    return (jnp.stack([cos_t, jnp.ones_like(cos_t)]), jnp.stack([sin_t, jnp.zeros_like(sin_t)]))


def _encoder_layer(x, p, wts, n_seq, seq_len):
    t = x.shape[0]
    tm = 1024
    tn = 512
    xb = x.astype(BF16)
    row_tile = lambda width: ((tm, width), lambda i, j: (i, j))

    w_in = wts["w_in"]
    o_q = 2 * CONV_CH
    o_gate = o_q + 3 * ATTN_W
    (u,) = _matmul(
        "proj_glu", [xb], [(0, w_in, _cols(0)), (0, w_in, _cols(CONV_CH // tn))], [],
        [(jax.ShapeDtypeStruct((t, CONV_CH), F32), *row_tile(tn))],
        _ep_glu, tm=tm, tn=tn, n_j=CONV_CH // tn)

    cos_t, sin_t = _rope_tables(seq_len)
    pos_blocks = seq_len // tm
    per_group = GROUP_W // tn
    tab = ((None, tm, HEAD_DIM), lambda i, j: (j // (2 * per_group), i % pos_blocks, 0))
    qkv_groups = []
    for g, d in enumerate(DILATIONS):
        col = lambda j, g=g: (o_q // tn + (j // per_group) * (ATTN_W // tn) + g * per_group
                              + j % per_group)
        (qkv_g,) = _matmul(
            f"proj_qkv_d{d}", [xb], [(0, w_in, col)], [(cos_t, *tab), (sin_t, *tab)],
            [(jax.ShapeDtypeStruct((3, d, t // d, GROUP_W), BF16), (None, d, tm // d, tn),
              lambda i, j: (j // per_group, 0, i, j % per_group))],
            functools.partial(_ep_qkv, heads_per_block=tn // HEAD_DIM, dilation=d),
            tm=tm, tn=tn, n_j=3 * per_group,
            scratch=() if d == 1 else (pltpu.VMEM((tn // LANES, tm // M_SPLIT, LANES), F32),))
        qkv_groups.append(qkv_g)

    (gates,) = _matmul(
        "proj_gates", [xb], [(0, w_in, _cols(o_gate // tn))], [],
        [(jax.ShapeDtypeStruct((t, 2 * D_MODEL), BF16), *row_tile(tn))],
        _ep_sigmoid, tm=tm, tn=tn, n_j=2 * D_MODEL // tn)

    u2 = _conv_module(u, wts["conv_w"], wts["conv_b"], wts["conv_ln_g"], wts["conv_ln_b"],
                      n_seq, seq_len)

    os_, lses = [], []
    for qkv_g, d in zip(qkv_groups, DILATIONS):
        o_g, lse_g = _attention_group(qkv_g, d, n_seq, seq_len)
        os_.append(o_g)
        lses.append(lse_g)
    attn = _combine_groups(os_, lses)

    n_dj = D_MODEL // tn
    (merged,) = _matmul(
        "merge", [u2, attn], [(0, wts["w_conv_out"], _cols(0)), (1, wts["w_attn_out"], _cols(0))],
        [(gates, *row_tile(tn)), (gates, (tm, tn), lambda i, j: (i, n_dj + j))],
        [(jax.ShapeDtypeStruct((t, D_MODEL), BF16), *row_tile(tn))],
        _ep_merge, tm=tm, tn=tn, n_j=n_dj)
    (pre1,) = _matmul(
        "out_proj", [merged], [(0, wts["w_out"], _cols(0))], [(x, *row_tile(tn))],
        [(jax.ShapeDtypeStruct((t, D_MODEL), F32), *row_tile(tn))],
        functools.partial(_ep_residual, scale=ALPHA), tm=tm, tn=tn, n_j=n_dj)
    x1, x1b = _layer_norm("ln1", pre1, wts["ln1_g"], wts["ln1_b"], (F32, BF16))

    tf = 256
    (act,) = _matmul(
        "ffn_up", [x1b], [(0, wts["w_gate_up"], _cols(0)), (0, wts["w_gate_up"], _cols(D_FF // tf))],
        [], [(jax.ShapeDtypeStruct((t, D_FF), BF16), *row_tile(tf))],
        _ep_swiglu, tm=tm, tn=tf, n_j=D_FF // tf)
    pb = p.astype(BF16)
    (res2,) = _matmul(
        "ple", [x1b, pb], [(0, wts["w_ple_gate"], _cols(0)), (1, wts["w_ple_proj"], _cols(0))],
        [(x1, *row_tile(tn))],
        [(jax.ShapeDtypeStruct((t, D_MODEL), F32), *row_tile(tn))],
        _ep_ple, tm=tm, tn=tn, n_j=n_dj)
    td_m, td_n = 512, 256
    (pre2,) = _matmul(
        "ffn_down", [act], [(0, wts["w_down"], _cols(0))],
        [(res2, (td_m, td_n), lambda i, j: (i, j))],
        [(jax.ShapeDtypeStruct((t, D_MODEL), F32), (td_m, td_n), lambda i, j: (i, j))],
        functools.partial(_ep_residual, scale=1.0), tm=td_m, tn=td_n, n_j=D_MODEL // td_n)
    (y,) = _layer_norm("ln2", pre2, wts["ln2_g"], wts["ln2_b"], (F32,))
    return y


def kernel(x_prompt, x_sample, p_prompt, p_sample, w_in, conv_w, conv_b, conv_ln_g, conv_ln_b,
           w_conv_out, w_attn_out, w_out, ln1_g, ln1_b, w_gate_up, w_down, w_ple_gate, w_ple_proj,
           ln2_g, ln2_b):
    depth = w_in.shape[0]
    layers = []
    for i in range(depth):
        layers.append({
            "w_in": w_in[i].astype(BF16),
            "conv_w": conv_w[i].reshape(CONV_WIDTH, CONV_CH),
            "conv_b": conv_b[i].reshape(1, CONV_CH),
            "conv_ln_g": conv_ln_g[i].reshape(1, CONV_CH),
            "conv_ln_b": conv_ln_b[i].reshape(1, CONV_CH),
            "w_conv_out": w_conv_out[i].astype(BF16),
            "w_attn_out": w_attn_out[i].astype(BF16),
            "w_out": w_out[i].astype(BF16),
            "ln1_g": ln1_g[i].reshape(1, D_MODEL),
            "ln1_b": ln1_b[i].reshape(1, D_MODEL),
            "w_gate_up": w_gate_up[i].astype(BF16),
            "w_down": w_down[i].astype(BF16),
            "w_ple_gate": w_ple_gate[i].astype(BF16),
            "w_ple_proj": w_ple_proj[i].astype(BF16),
            "ln2_g": ln2_g[i].reshape(1, D_MODEL),
            "ln2_b": ln2_b[i].reshape(1, D_MODEL),
        })
    outs = []
    for x, p in ((x_prompt, p_prompt), (x_sample, p_sample)):
        n_seq, seq_len, _ = x.shape
        h = x.reshape(n_seq * seq_len, D_MODEL)
        for i, wts in enumerate(layers):
            h = _encoder_layer(h, p[i].reshape(n_seq * seq_len, PLE_DIM), wts, n_seq, seq_len)
        outs.append(h.reshape(n_seq, seq_len, D_MODEL))
    return tuple(outs)
```

```python
import functools
import math

import jax
import jax.numpy as jnp
from jax import lax
from jax.experimental import pallas as pl
from jax.experimental.pallas import tpu as pltpu

D_MODEL = 4096
CONV_CH = 2048
CONV_WIDTH = 31
CONV_HALF = CONV_WIDTH // 2
HEAD_DIM = 128
N_SLOTS = 8
DILATIONS = (1, 4, 16)
RADIUS = 64
N_GROUPS = len(DILATIONS)
GROUP_W = N_SLOTS * HEAD_DIM
ATTN_W = N_GROUPS * GROUP_W
ROPE_DIM = HEAD_DIM // 4
ROPE_HALF = ROPE_DIM // 2
ROPE_THETA = 500000.0
D_FF = 11008
PLE_DIM = 256
ALPHA = 2.0 ** 0.25
LN_EPS = 1e-5
NEG_INF = -1e30

V7X_VMEM_LIMIT_CAP = 56 * 1024 * 1024
M_SPLIT = 4
MM_TM = 1024
MM_TN = 1024
MM_TN_F32 = 1024
FFN_UP_TM, FFN_UP_TN = 2048, 256
FFN_DOWN_TM, FFN_DOWN_TN = 512, 512
LANES = 128
SUBLANES = 8

F32 = jnp.float32
BF16 = jnp.bfloat16


def _nbytes(shape, dtype):
    return math.prod(shape) * jnp.dtype(dtype).itemsize


def _vmem_limit(block_bytes, temp_bytes):
    est = 2 * block_bytes + temp_bytes + (4 << 20)
    return int(min(max(est, 16 << 20), V7X_VMEM_LIMIT_CAP))


def _mm_body(*refs, n_lhs, dots, n_extra, n_out, m_split, epilogue):
    lhs = refs[:n_lhs]
    rhs = refs[n_lhs:n_lhs + len(dots)]
    extra = refs[n_lhs + len(dots):n_lhs + len(dots) + n_extra]
    outs = refs[n_lhs + len(dots) + n_extra:n_lhs + len(dots) + n_extra + n_out]
    scratch = refs[n_lhs + len(dots) + n_extra + n_out:]
    ws = [None if r.dtype == BF16 else r[...].astype(BF16) for r in rhs]
    rows = lhs[0].shape[0] // m_split
    for c in range(m_split):
        rs = slice(c * rows, (c + 1) * rows)
        accs = [jnp.dot(lhs[li][rs, :], r[...] if w is None else w, preferred_element_type=F32)
                for li, r, w in zip(dots, rhs, ws)]
        epilogue(rs, accs, extra, outs, scratch)


def _matmul(name, lhs, rhs, extras, outs, epilogue, *, tm, tn, n_j, m_split=M_SPLIT, scratch=(),
            lhs_single_buffer=False):
    m = lhs[0].shape[0]
    grid = (m // tm, n_j)
    in_specs, args, blk = [], [], 0
    once = 0
    for a in lhs:
        mode = dict(pipeline_mode=pl.Buffered(1)) if lhs_single_buffer else {}
        in_specs.append(pl.BlockSpec((tm, a.shape[1]), lambda i, j: (i, 0), **mode))
        args.append(a)
        if lhs_single_buffer:
            once += _nbytes((tm, a.shape[1]), a.dtype)
        else:
            blk += _nbytes((tm, a.shape[1]), a.dtype)
    for _, w, col in rhs:
        in_specs.append(pl.BlockSpec((w.shape[0], tn), lambda i, j, col=col: (0, col(j))))
        args.append(w)
        blk += _nbytes((w.shape[0], tn), w.dtype)
        if w.dtype != BF16:
            once += _nbytes((w.shape[0], tn), BF16)
    for a, bs, im in extras:
        in_specs.append(pl.BlockSpec(bs, im))
        args.append(a)
        blk += _nbytes([b for b in bs if b is not None], a.dtype)
    out_shape, out_specs = [], []
    for sds, bs, im in outs:
        out_shape.append(sds)
        out_specs.append(pl.BlockSpec(bs, im))
        blk += _nbytes([b for b in bs if b is not None], sds.dtype)
    temp = ((len(rhs) + 4) * (tm // m_split) * tn * 4 + once
            + sum(_nbytes(s.shape, s.dtype) for s in scratch))
    body = functools.partial(_mm_body, n_lhs=len(lhs), dots=tuple(r[0] for r in rhs),
                             n_extra=len(extras), n_out=len(outs), m_split=m_split, epilogue=epilogue)
    return pl.pallas_call(
        body,
        out_shape=out_shape,
        grid=grid,
        in_specs=in_specs,
        out_specs=out_specs,
        scratch_shapes=list(scratch),
        compiler_params=pltpu.CompilerParams(
            dimension_semantics=("parallel", "arbitrary"),
            vmem_limit_bytes=_vmem_limit(blk, temp)),
        name=name,
    )(*args)


def _cols(offset):
    return lambda j: offset + j


def _ep_glu(rs, accs, extra, outs, scratch):
    a, b = accs
    outs[0][rs, :] = a * jax.nn.sigmoid(b)


def _ep_sigmoid(rs, accs, extra, outs, scratch):
    outs[0][rs, :] = jax.nn.sigmoid(accs[0]).astype(outs[0].dtype)


def _ep_qkv(rs, accs, extra, outs, scratch, *, heads_per_block, dilation):
    acc = accs[0]
    cos_ref, sin_ref = extra
    o_ref = outs[0]
    cos = jnp.concatenate([cos_ref[rs, :]] * heads_per_block, axis=1)
    sin = jnp.concatenate([sin_ref[rs, :]] * heads_per_block, axis=1)
    width = acc.shape[1]
    lane = lax.broadcasted_iota(jnp.int32, acc.shape, 1) % HEAD_DIM
    partner = jnp.where(lane < ROPE_HALF,
                        pltpu.roll(acc, width - ROPE_HALF, axis=1),
                        pltpu.roll(acc, ROPE_HALF, axis=1))
    val = acc * cos + partner * sin
    ro = slice(rs.start // dilation, rs.stop // dilation)
    if dilation == 1:
        o_ref[0, ro, :] = val.astype(o_ref.dtype)
    else:
        stage = scratch[0]
        n = val.shape[0] // dilation
        for c in range(width // LANES):
            cs = slice(c * LANES, (c + 1) * LANES)
            stage[c] = val[:, cs]
            for r in range(dilation):
                o_ref[r, ro, cs] = stage[c, pl.ds(r, n, stride=dilation), :].astype(o_ref.dtype)


def _ep_merge(rs, accs, extra, outs, scratch):
    conv, attn = accs
    gc_ref, ga_ref = extra
    merged = gc_ref[rs, :].astype(F32) * conv + ga_ref[rs, :].astype(F32) * attn
    outs[0][rs, :] = merged.astype(outs[0].dtype)


def _ep_residual(rs, accs, extra, outs, scratch, *, scale):
    outs[0][rs, :] = scale * extra[0][rs, :] + accs[0]


def _ep_swiglu(rs, accs, extra, outs, scratch):
    g, u = accs
    outs[0][rs, :] = (g * jax.nn.sigmoid(g) * u).astype(outs[0].dtype)


def _ep_ple(rs, accs, extra, outs, scratch):
    gate, proj = accs
    outs[0][rs, :] = ALPHA * extra[0][rs, :] + jax.nn.sigmoid(gate) * proj


CONV_TS = 256
CONV_HALO = 16
CONV_CW = 128
CONV_ROWS = 128


def _conv_body(prev_ref, cur_ref, next_ref, w_ref, b_ref, g_ref, beta_ref, o_ref, xp_ref, y_ref, sh_ref):
    lb = pl.program_id(1)
    n_lb = pl.num_programs(1)
    ts = cur_ref.shape[0]
    prev = jnp.where(lb > 0, prev_ref[...], 0.0)
    nxt = jnp.where(lb < n_lb - 1, next_ref[...], 0.0)
    n_chunks = CONV_CH // CONV_CW
    for c in range(n_chunks):
        cs = slice(c * CONV_CW, (c + 1) * CONV_CW)
        xp_ref[c, 0:CONV_HALO, :] = prev[:, cs]
        xp_ref[c, CONV_HALO:CONV_HALO + ts, :] = cur_ref[:, cs]
        xp_ref[c, CONV_HALO + ts:, :] = nxt[:, cs]
    base = CONV_HALO - CONV_HALF
    span = ts + (base + CONV_WIDTH - 1) // SUBLANES * SUBLANES
    rows = min(ts, CONV_ROWS)

    def chunk(c, carry):
        for k in range(SUBLANES):
            sh_ref[k] = xp_ref[c, k:k + span, :]
        w_c = w_ref[c]
        bias = jnp.broadcast_to(b_ref[c], (rows // SUBLANES, SUBLANES, CONV_CW))
        for r0 in range(0, ts, rows):
            acc = bias
            for tap in range(CONV_WIDTH):
                off = base + tap
                lo = off // SUBLANES * SUBLANES + r0
                wb = jnp.broadcast_to(w_c[tap:tap + 1, :], (SUBLANES, CONV_CW))
                win = sh_ref[off % SUBLANES, lo:lo + rows, :]
                acc = acc + win.reshape(rows // SUBLANES, SUBLANES, CONV_CW) * wb[None]
            y_ref[c, r0:r0 + rows, :] = acc.reshape(rows, CONV_CW)
        return carry

    lax.fori_loop(0, n_chunks, chunk, 0)

    tot = y_ref[0]
    for c in range(1, n_chunks):
        tot = tot + y_ref[c]
    mu = jnp.sum(tot, axis=-1, keepdims=True) * (1.0 / CONV_CH)
    sq = jnp.zeros_like(tot)
    for c in range(n_chunks):
        yc = y_ref[c] - mu
        sq = sq + yc * yc
    var = jnp.sum(sq, axis=-1, keepdims=True) * (1.0 / CONV_CH)
    rstd = lax.rsqrt(var + LN_EPS)
    for c in range(n_chunks):
        cs = slice(c * CONV_CW, (c + 1) * CONV_CW)
        z = (y_ref[c] - mu) * rstd * g_ref[:, cs] + beta_ref[:, cs]
        o_ref[:, cs] = (z * jax.nn.sigmoid(z)).astype(o_ref.dtype)


def _conv_module(u, conv_w, conv_b, ln_g, ln_b, n_seq, seq_len):
    t = u.shape[0]
    ts = CONV_TS
    n_lb = seq_len // ts
    r = ts // CONV_HALO
    n_halo_blocks = t // CONV_HALO
    n_chunks = CONV_CH // CONV_CW
    conv_w = conv_w.reshape(CONV_WIDTH, n_chunks, CONV_CW).transpose(1, 0, 2)
    conv_b = conv_b.reshape(n_chunks, 1, CONV_CW)

    def cur_map(b, lb):
        return (b * n_lb + lb, 0)

    def prev_map(b, lb):
        return (jnp.maximum((b * n_lb + lb) * r - 1, 0), 0)

    def next_map(b, lb):
        return (jnp.minimum((b * n_lb + lb + 1) * r, n_halo_blocks - 1), 0)

    vec = pl.BlockSpec((1, CONV_CH), lambda b, lb: (0, 0))
    blk = _nbytes((ts + 2 * CONV_HALO, CONV_CH), F32) + _nbytes((ts, CONV_CH), BF16)
    return pl.pallas_call(
        _conv_body,
        out_shape=jax.ShapeDtypeStruct((t, CONV_CH), BF16),
        grid=(n_seq, n_lb),
        in_specs=[
            pl.BlockSpec((CONV_HALO, CONV_CH), prev_map),
            pl.BlockSpec((ts, CONV_CH), cur_map),
            pl.BlockSpec((CONV_HALO, CONV_CH), next_map),
            pl.BlockSpec((n_chunks, CONV_WIDTH, CONV_CW), lambda b, lb: (0, 0, 0)),
            pl.BlockSpec((n_chunks, 1, CONV_CW), lambda b, lb: (0, 0, 0)),
            vec, vec,
        ],
        out_specs=pl.BlockSpec((ts, CONV_CH), cur_map),
        scratch_shapes=[pltpu.VMEM((n_chunks, ts + 2 * CONV_HALO, CONV_CW), F32),
                        pltpu.VMEM((n_chunks, ts, CONV_CW), F32),
                        pltpu.VMEM((SUBLANES, ts + 2 * CONV_HALO - SUBLANES, CONV_CW), F32)],
        compiler_params=pltpu.CompilerParams(
            dimension_semantics=("parallel", "arbitrary"),
            vmem_limit_bytes=_vmem_limit(blk, 10 * _nbytes((ts + 2 * CONV_HALO, CONV_CH), F32))),
        name="conv_module",
    )(u, u, u, conv_w, conv_b, ln_g, ln_b)


ATT_BQ = 512
ATT_SUB = 128


def _attn_body(q_ref, kp_ref, kc_ref, kn_ref, vp_ref, vc_ref, vn_ref, o_ref, lse_ref, k_buf, v_buf):
    lb = pl.program_id(1)
    n_lb = pl.num_programs(1)
    bq = q_ref.shape[0]
    sub = min(ATT_SUB, bq)
    n_sub = bq // sub
    nk = sub + 2 * RADIUS
    k_buf[0:RADIUS, :] = kp_ref[...]
    k_buf[RADIUS:RADIUS + bq, :] = kc_ref[...]
    k_buf[RADIUS + bq:, :] = kn_ref[...]
    v_buf[0:RADIUS, :] = vp_ref[...]
    v_buf[RADIUS:RADIUS + bq, :] = vc_ref[...]
    v_buf[RADIUS + bq:, :] = vn_ref[...]

    qi = lax.broadcasted_iota(jnp.int32, (sub, nk), 0)
    kt = lax.broadcasted_iota(jnp.int32, (sub, nk), 1)
    band = (kt >= qi) & (kt <= qi + 2 * RADIUS)
    first = band & ((kt >= RADIUS) | (lb > 0))
    last = band & ((kt < RADIUS + sub) | (lb < n_lb - 1))
    scale = 1.0 / math.sqrt(HEAD_DIM)

    for sb in range(n_sub):
        valid = band
        if sb == 0:
            valid = first
        if sb == n_sub - 1:
            valid = valid & last
        qs = slice(sb * sub, (sb + 1) * sub)
        ks = slice(sb * sub, sb * sub + nk)
        for h in range(N_SLOTS):
            hs = slice(h * HEAD_DIM, (h + 1) * HEAD_DIM)
            s = lax.dot_general(q_ref[qs, hs], k_buf[ks, hs], (((1,), (1,)), ((), ())),
                                preferred_element_type=F32)
            s = jnp.where(valid, s * scale, NEG_INF)
            m = jnp.max(s, axis=-1, keepdims=True)
            e = jnp.exp(s - m)
            den = jnp.sum(e, axis=-1, keepdims=True)
            pv = jnp.dot(e.astype(BF16), v_buf[ks, hs], preferred_element_type=F32)
            o_ref[qs, hs] = pv / den
            lse_ref[qs, hs] = jnp.broadcast_to(m + jnp.log(den), (sub, HEAD_DIM))


def _attention_group(qkv, dilation, n_seq, seq_len):
    _, d, rows, _ = qkv.shape
    sub_len = seq_len // d
    bq = min(ATT_BQ, sub_len)
    n_lb = sub_len // bq
    hr = bq // RADIUS
    n_halo = rows // RADIUS

    def cur(which):
        return pl.BlockSpec((None, None, bq, GROUP_W),
                            lambda b, lb, r: (which, r, b * n_lb + lb, 0))

    def halo_prev(which):
        return pl.BlockSpec((None, None, RADIUS, GROUP_W),
                            lambda b, lb, r: (which, r, jnp.maximum((b * n_lb + lb) * hr - 1, 0), 0))

    def halo_next(which):
        return pl.BlockSpec((None, None, RADIUS, GROUP_W),
                            lambda b, lb, r: (which, r, jnp.minimum((b * n_lb + lb + 1) * hr, n_halo - 1), 0))

    out_spec = pl.BlockSpec((None, bq, GROUP_W), lambda b, lb, r: (r, b * n_lb + lb, 0))
    nk = bq + 2 * RADIUS
    blk = 3 * _nbytes((nk, GROUP_W), BF16) + 2 * _nbytes((bq, GROUP_W), F32)
    return pl.pallas_call(
        _attn_body,
        out_shape=[jax.ShapeDtypeStruct((d, rows, GROUP_W), F32)] * 2,
        grid=(n_seq, n_lb, d),
        in_specs=[cur(0), halo_prev(1), cur(1), halo_next(1), halo_prev(2), cur(2), halo_next(2)],
        out_specs=[out_spec, out_spec],
        scratch_shapes=[pltpu.VMEM((nk, GROUP_W), BF16), pltpu.VMEM((nk, GROUP_W), BF16)],
        compiler_params=pltpu.CompilerParams(
            dimension_semantics=("parallel", "arbitrary", "arbitrary"),
            vmem_limit_bytes=_vmem_limit(blk, 2 * _nbytes((nk, GROUP_W), BF16) + (8 << 20))),
        name=f"band_attention_d{d}",
    )(qkv, qkv, qkv, qkv, qkv, qkv, qkv)


COMBINE_ROWS = 512


def _combine_body(*refs):
    o_refs = refs[0:N_GROUPS]
    l_refs = refs[N_GROUPS:2 * N_GROUPS]
    out_ref = refs[2 * N_GROUPS]
    scratch = list(refs[2 * N_GROUPS + 1:])

    def position_major(ref):
        d, n, w = ref.shape
        if d == 1:
            return ref[0]
        buf = scratch.pop()
        for c in range(w // LANES):
            for r in range(d):
                buf[c, pl.ds(r, n, stride=d), :] = ref[r, :, c * LANES:(c + 1) * LANES]
        return jnp.concatenate([buf[c] for c in range(w // LANES)], axis=1)

    os_ = [position_major(r) for r in o_refs]
    ls = [position_major(r) for r in l_refs]
    m = functools.reduce(jnp.maximum, ls)
    es = [jnp.exp(l - m) for l in ls]
    num = sum(e * o for e, o in zip(es, os_))
    out_ref[...] = (num / sum(es)).astype(out_ref.dtype)


def _combine_groups(os_, lses):
    t = os_[0].shape[0] * os_[0].shape[1]
    tr = COMBINE_ROWS
    specs = [pl.BlockSpec((a.shape[0], tr // a.shape[0], GROUP_W), lambda i: (0, i, 0))
             for a in (*os_, *lses)]
    n_scratch = sum(1 for a in (*os_, *lses) if a.shape[0] > 1)
    return pl.pallas_call(
        _combine_body,
        out_shape=jax.ShapeDtypeStruct((t, GROUP_W), BF16),
        grid=(t // tr,),
        in_specs=specs,
        out_specs=pl.BlockSpec((tr, GROUP_W), lambda i: (i, 0)),
        scratch_shapes=[pltpu.VMEM((GROUP_W // LANES, tr, LANES), F32)] * n_scratch,
        compiler_params=pltpu.CompilerParams(
            dimension_semantics=("parallel",),
            vmem_limit_bytes=_vmem_limit(7 * _nbytes((tr, GROUP_W), F32),
                                         (n_scratch + 6) * _nbytes((tr, GROUP_W), F32))),
        name="combine_groups",
    )(*os_, *lses)


def _ln_body(x_ref, g_ref, b_ref, *out_refs):
    x = x_ref[...]
    mu = jnp.mean(x, axis=-1, keepdims=True)
    xc = x - mu
    var = jnp.mean(xc * xc, axis=-1, keepdims=True)
    y = xc * lax.rsqrt(var + LN_EPS) * g_ref[...] + b_ref[...]
    for o in out_refs:
        o[...] = y.astype(o.dtype)


def _layer_norm(name, x, g, b, out_dtypes):
    t, dm = x.shape
    tr = 256
    spec = pl.BlockSpec((tr, dm), lambda i: (i, 0))
    vec = pl.BlockSpec((1, dm), lambda i: (0, 0))
    return pl.pallas_call(
        _ln_body,
        out_shape=[jax.ShapeDtypeStruct((t, dm), dt) for dt in out_dtypes],
        grid=(t // tr,),
        in_specs=[spec, vec, vec],
        out_specs=[spec] * len(out_dtypes),
        compiler_params=pltpu.CompilerParams(
            dimension_semantics=("parallel",),
            vmem_limit_bytes=_vmem_limit((1 + len(out_dtypes)) * _nbytes((tr, dm), F32),
                                         4 * _nbytes((tr, dm), F32))),
        name=name,
    )(x, g, b)


def _rope_tables(seq_len):
    pos = jnp.arange(seq_len, dtype=F32)
    inv = ROPE_THETA ** (-jnp.arange(0, ROPE_DIM, 2, dtype=F32) / ROPE_DIM)
    ang = pos[:, None] * inv[None, :]
    cos, sin = jnp.cos(ang), jnp.sin(ang)
    pad1 = jnp.ones((seq_len, HEAD_DIM - ROPE_DIM), F32)
    pad0 = jnp.zeros((seq_len, HEAD_DIM - ROPE_DIM), F32)
    cos_t = jnp.concatenate([cos, cos, pad1], axis=1)
    sin_t = jnp.concatenate([-sin, sin, pad0], axis=1)
    return (jnp.stack([cos_t, jnp.ones_like(cos_t)]), jnp.stack([sin_t, jnp.zeros_like(sin_t)]))


def _encoder_layer(x, p, wts, n_seq, seq_len):
    t = x.shape[0]
    tm = MM_TM
    xb = x.astype(BF16)

    def tile(rows, width):
        return ((rows, width), lambda i, j: (i, j))

    w_in = wts["w_in"]
    o_q = 2 * CONV_CH
    o_gate = o_q + 3 * ATTN_W
    tg = min(MM_TN // 2, CONV_CH)
    (u,) = _matmul(
        "proj_glu", [xb], [(0, w_in, _cols(0)), (0, w_in, _cols(CONV_CH // tg))], [],
        [(jax.ShapeDtypeStruct((t, CONV_CH), F32), *tile(tm, tg))],
        _ep_glu, tm=tm, tn=tg, n_j=CONV_CH // tg)

    cos_t, sin_t = _rope_tables(seq_len)
    pos_blocks = seq_len // tm
    tq = GROUP_W
    tab = ((None, tm, HEAD_DIM), lambda i, j: (j // 2, i % pos_blocks, 0))
    qkv_groups = []
    for g, d in enumerate(DILATIONS):
        col = lambda j, g=g: o_q // tq + j * N_GROUPS + g
        (qkv_g,) = _matmul(
            f"proj_qkv_d{d}", [xb], [(0, w_in, col)], [(cos_t, *tab), (sin_t, *tab)],
            [(jax.ShapeDtypeStruct((3, d, t // d, GROUP_W), BF16), (None, d, tm // d, tq),
              lambda i, j: (j, 0, i, 0))],
            functools.partial(_ep_qkv, heads_per_block=tq // HEAD_DIM, dilation=d),
            tm=tm, tn=tq, n_j=3,
            scratch=() if d == 1 else (pltpu.VMEM((tq // LANES, tm // M_SPLIT, LANES), F32),))
        qkv_groups.append(qkv_g)

    tw = min(MM_TN, D_MODEL)
    n_dj = D_MODEL // tw
    (gates,) = _matmul(
        "proj_gates", [xb], [(0, w_in, _cols(o_gate // tw))], [],
        [(jax.ShapeDtypeStruct((t, 2 * D_MODEL), BF16), *tile(tm, tw))],
        _ep_sigmoid, tm=tm, tn=tw, n_j=2 * n_dj)

    u2 = _conv_module(u, wts["conv_w"], wts["conv_b"], wts["conv_ln_g"], wts["conv_ln_b"],
                      n_seq, seq_len)

    os_, lses = [], []
    for qkv_g, d in zip(qkv_groups, DILATIONS):
        o_g, lse_g = _attention_group(qkv_g, d, n_seq, seq_len)
        os_.append(o_g)
        lses.append(lse_g)
    attn = _combine_groups(os_, lses)

    (merged,) = _matmul(
        "merge", [u2, attn], [(0, wts["w_conv_out"], _cols(0)), (1, wts["w_attn_out"], _cols(0))],
        [(gates, *tile(tm, tw)), (gates, (tm, tw), lambda i, j: (i, n_dj + j))],
        [(jax.ShapeDtypeStruct((t, D_MODEL), BF16), *tile(tm, tw))],
        _ep_merge, tm=tm, tn=tw, n_j=n_dj)
    tr = min(MM_TN_F32, D_MODEL)
    (pre1,) = _matmul(
        "out_proj", [merged], [(0, wts["w_out"], _cols(0))], [(x, *tile(tm, tr))],
        [(jax.ShapeDtypeStruct((t, D_MODEL), F32), *tile(tm, tr))],
        functools.partial(_ep_residual, scale=ALPHA), tm=tm, tn=tr, n_j=D_MODEL // tr)
    x1, x1b = _layer_norm("ln1", pre1, wts["ln1_g"], wts["ln1_b"], (F32, BF16))

    tf = FFN_UP_TN
    (act,) = _matmul(
        "ffn_up", [x1b], [(0, wts["w_gate_up"], _cols(0)), (0, wts["w_gate_up"], _cols(D_FF // tf))],
        [], [(jax.ShapeDtypeStruct((t, D_FF), BF16), *tile(FFN_UP_TM, tf))],
        _ep_swiglu, tm=FFN_UP_TM, tn=tf, n_j=D_FF // tf, lhs_single_buffer=True)
    pb = p.astype(BF16)
    (res2,) = _matmul(
        "ple", [x1b, pb], [(0, wts["w_ple_gate"], _cols(0)), (1, wts["w_ple_proj"], _cols(0))],
        [(x1, *tile(tm, tr))],
        [(jax.ShapeDtypeStruct((t, D_MODEL), F32), *tile(tm, tr))],
        _ep_ple, tm=tm, tn=tr, n_j=D_MODEL // tr)
    td_m, td_n = FFN_DOWN_TM, min(FFN_DOWN_TN, D_MODEL)
    (pre2,) = _matmul(
        "ffn_down", [act], [(0, wts["w_down"], _cols(0))],
        [(res2, *tile(td_m, td_n))],
        [(jax.ShapeDtypeStruct((t, D_MODEL), F32), *tile(td_m, td_n))],
        functools.partial(_ep_residual, scale=1.0), tm=td_m, tn=td_n, n_j=D_MODEL // td_n)
    (y,) = _layer_norm("ln2", pre2, wts["ln2_g"], wts["ln2_b"], (F32,))
    return y


def kernel(x_prompt, x_sample, p_prompt, p_sample, w_in, conv_w, conv_b, conv_ln_g, conv_ln_b,
           w_conv_out, w_attn_out, w_out, ln1_g, ln1_b, w_gate_up, w_down, w_ple_gate, w_ple_proj,
           ln2_g, ln2_b):
    depth = w_in.shape[0]
    layers = []
    for i in range(depth):
        layers.append({
            "w_in": w_in[i].astype(BF16),
            "conv_w": conv_w[i].reshape(CONV_WIDTH, CONV_CH),
            "conv_b": conv_b[i].reshape(1, CONV_CH),
            "conv_ln_g": conv_ln_g[i].reshape(1, CONV_CH),
            "conv_ln_b": conv_ln_b[i].reshape(1, CONV_CH),
            "w_conv_out": w_conv_out[i].astype(BF16),
            "w_attn_out": w_attn_out[i].astype(BF16),
            "w_out": w_out[i].astype(BF16),
            "ln1_g": ln1_g[i].reshape(1, D_MODEL),
            "ln1_b": ln1_b[i].reshape(1, D_MODEL),
            "w_gate_up": w_gate_up[i],
            "w_down": w_down[i].astype(BF16),
            "w_ple_gate": w_ple_gate[i].astype(BF16),
            "w_ple_proj": w_ple_proj[i].astype(BF16),
            "ln2_g": ln2_g[i].reshape(1, D_MODEL),
            "ln2_b": ln2_b[i].reshape(1, D_MODEL),
        })
    outs = []
    for x, p in ((x_prompt, p_prompt), (x_sample, p_sample)):
        n_seq, seq_len, _ = x.shape
        h = x.reshape(n_seq * seq_len, D_MODEL)
        for i, wts in enumerate(layers):
            h = _encoder_layer(h, p[i].reshape(n_seq * seq_len, PLE_DIM), wts, n_seq, seq_len)
        outs.append(h.reshape(n_seq, seq_len, D_MODEL))
    return tuple(outs)
```

```python
import functools
import math

import jax
import jax.numpy as jnp
from jax import lax
from jax.experimental import pallas as pl
from jax.experimental.pallas import tpu as pltpu

D_MODEL = 4096
CONV_CH = 2048
CONV_WIDTH = 31
CONV_HALF = CONV_WIDTH // 2
HEAD_DIM = 128
N_SLOTS = 8
DILATIONS = (1, 4, 16)
RADIUS = 64
N_GROUPS = len(DILATIONS)
GROUP_W = N_SLOTS * HEAD_DIM
ATTN_W = N_GROUPS * GROUP_W
ROPE_DIM = HEAD_DIM // 4
ROPE_HALF = ROPE_DIM // 2
ROPE_THETA = 500000.0
D_FF = 11008
PLE_DIM = 256
ALPHA = 2.0 ** 0.25
LN_EPS = 1e-5
NEG_INF = -1e30

V7X_VMEM_LIMIT_CAP = 56 * 1024 * 1024
M_SPLIT = 4
MERGE_M_SPLIT = 2
MM_TM = 1024
MM_TN = 1024
MM_TN_F32 = 1024
FFN_UP_TM, FFN_UP_TN = 2048, 256
FFN_DOWN_TN = 512
FFN_DOWN_K_SPLIT = 2
LANES = 128
SUBLANES = 8

F32 = jnp.float32
BF16 = jnp.bfloat16


def _nbytes(shape, dtype):
    return math.prod(shape) * jnp.dtype(dtype).itemsize


def _vmem_limit(block_bytes, temp_bytes):
    est = 2 * block_bytes + temp_bytes + (4 << 20)
    return int(min(max(est, 16 << 20), V7X_VMEM_LIMIT_CAP))


def _mm_body(*refs, n_lhs, dots, n_extra, n_out, m_split, epilogue):
    lhs = refs[:n_lhs]
    rhs = refs[n_lhs:n_lhs + len(dots)]
    extra = refs[n_lhs + len(dots):n_lhs + len(dots) + n_extra]
    outs = refs[n_lhs + len(dots) + n_extra:n_lhs + len(dots) + n_extra + n_out]
    scratch = refs[n_lhs + len(dots) + n_extra + n_out:]
    ws = [None if r.dtype == BF16 else r[...].astype(BF16) for r in rhs]
    rows = lhs[0].shape[0] // m_split
    for c in range(m_split):
        rs = slice(c * rows, (c + 1) * rows)
        accs = [jnp.dot(lhs[li][rs, :], r[...] if w is None else w, preferred_element_type=F32)
                for li, r, w in zip(dots, rhs, ws)]
        epilogue(rs, accs, extra, outs, scratch)


def _matmul(name, lhs, rhs, extras, outs, epilogue, *, tm, tn, n_j, m_split=M_SPLIT, scratch=(),
            lhs_single_buffer=False, k_block=None):
    m = lhs[0].shape[0]
    grid = (m // tm, n_j)
    in_specs, args, blk = [], [], 0
    once = 0
    for a in lhs:
        mode = dict(pipeline_mode=pl.Buffered(1)) if lhs_single_buffer else {}
        kb, ki = k_block or (a.shape[1], 0)
        in_specs.append(pl.BlockSpec((tm, kb), lambda i, j, ki=ki: (i, ki), **mode))
        args.append(a)
        if lhs_single_buffer:
            once += _nbytes((tm, kb), a.dtype)
        else:
            blk += _nbytes((tm, kb), a.dtype)
    for _, w, col in rhs:
        kb, ki = k_block or (w.shape[0], 0)
        in_specs.append(pl.BlockSpec((kb, tn), lambda i, j, col=col, ki=ki: (ki, col(j))))
        args.append(w)
        blk += _nbytes((kb, tn), w.dtype)
        if w.dtype != BF16:
            once += _nbytes((kb, tn), BF16)
    for a, bs, im in extras:
        in_specs.append(pl.BlockSpec(bs, im))
        args.append(a)
        blk += _nbytes([b for b in bs if b is not None], a.dtype)
    out_shape, out_specs = [], []
    for sds, bs, im in outs:
        out_shape.append(sds)
        out_specs.append(pl.BlockSpec(bs, im))
        blk += _nbytes([b for b in bs if b is not None], sds.dtype)
    temp = ((len(rhs) + 4) * (tm // m_split) * tn * 4 + once
            + sum(_nbytes(s.shape, s.dtype) for s in scratch))
    body = functools.partial(_mm_body, n_lhs=len(lhs), dots=tuple(r[0] for r in rhs),
                             n_extra=len(extras), n_out=len(outs), m_split=m_split, epilogue=epilogue)
    return pl.pallas_call(
        body,
        out_shape=out_shape,
        grid=grid,
        in_specs=in_specs,
        out_specs=out_specs,
        scratch_shapes=list(scratch),
        compiler_params=pltpu.CompilerParams(
            dimension_semantics=("parallel", "arbitrary"),
            vmem_limit_bytes=_vmem_limit(blk, temp)),
        name=name,
    )(*args)


def _cols(offset):
    return lambda j: offset + j


def _ep_glu(rs, accs, extra, outs, scratch):
    a, b = accs
    outs[0][rs, :] = a * jax.nn.sigmoid(b)


def _ep_sigmoid(rs, accs, extra, outs, scratch):
    outs[0][rs, :] = jax.nn.sigmoid(accs[0]).astype(outs[0].dtype)


def _ep_qkv(rs, accs, extra, outs, scratch, *, heads_per_block, dilation):
    acc = accs[0]
    cos_ref, sin_ref = extra
    o_ref = outs[0]
    cos = jnp.concatenate([cos_ref[rs, :]] * heads_per_block, axis=1)
    sin = jnp.concatenate([sin_ref[rs, :]] * heads_per_block, axis=1)
    width = acc.shape[1]
    lane = lax.broadcasted_iota(jnp.int32, acc.shape, 1) % HEAD_DIM
    partner = jnp.where(lane < ROPE_HALF,
                        pltpu.roll(acc, width - ROPE_HALF, axis=1),
                        pltpu.roll(acc, ROPE_HALF, axis=1))
    val = acc * cos + partner * sin
    ro = slice(rs.start // dilation, rs.stop // dilation)
    if dilation == 1:
        o_ref[0, ro, :] = val.astype(o_ref.dtype)
    else:
        stage = scratch[0]
        n = val.shape[0] // dilation
        for c in range(width // LANES):
            cs = slice(c * LANES, (c + 1) * LANES)
            stage[c] = val[:, cs]
            for r in range(dilation):
                o_ref[r, ro, cs] = stage[c, pl.ds(r, n, stride=dilation), :].astype(o_ref.dtype)


def _ep_merge(rs, accs, extra, outs, scratch):
    conv, attn = accs
    gc_ref, ga_ref = extra
    merged = gc_ref[rs, :].astype(F32) * conv + ga_ref[rs, :].astype(F32) * attn
    outs[0][rs, :] = merged.astype(outs[0].dtype)


def _ep_residual(rs, accs, extra, outs, scratch, *, scale):
    outs[0][rs, :] = scale * extra[0][rs, :] + accs[0]


def _ep_swiglu(rs, accs, extra, outs, scratch):
    g, u = accs
    outs[0][rs, :] = (g * jax.nn.sigmoid(g) * u).astype(outs[0].dtype)


def _ep_ple(rs, accs, extra, outs, scratch):
    gate, proj = accs
    outs[0][rs, :] = ALPHA * extra[0][rs, :] + jax.nn.sigmoid(gate) * proj


CONV_TS = 256
CONV_HALO = 16
CONV_CW = 128
CONV_ROWS = 128


def _conv_body(prev_ref, cur_ref, next_ref, w_ref, b_ref, g_ref, beta_ref, o_ref, xp_ref, y_ref, sh_ref):
    lb = pl.program_id(1)
    n_lb = pl.num_programs(1)
    ts = cur_ref.shape[0]
    prev = jnp.where(lb > 0, prev_ref[...], 0.0)
    nxt = jnp.where(lb < n_lb - 1, next_ref[...], 0.0)
    n_chunks = CONV_CH // CONV_CW
    for c in range(n_chunks):
        cs = slice(c * CONV_CW, (c + 1) * CONV_CW)
        xp_ref[c, 0:CONV_HALO, :] = prev[:, cs]
        xp_ref[c, CONV_HALO:CONV_HALO + ts, :] = cur_ref[:, cs]
        xp_ref[c, CONV_HALO + ts:, :] = nxt[:, cs]
    base = CONV_HALO - CONV_HALF
    span = ts + (base + CONV_WIDTH - 1) // SUBLANES * SUBLANES
    rows = min(ts, CONV_ROWS)

    def chunk(c, carry):
        for k in range(SUBLANES):
            sh_ref[k] = xp_ref[c, k:k + span, :]
        w_c = w_ref[c]
        bias = jnp.broadcast_to(b_ref[c], (rows // SUBLANES, SUBLANES, CONV_CW))
        for r0 in range(0, ts, rows):
            acc = bias
            for tap in range(CONV_WIDTH):
                off = base + tap
                lo = off // SUBLANES * SUBLANES + r0
                wb = jnp.broadcast_to(w_c[tap:tap + 1, :], (SUBLANES, CONV_CW))
                win = sh_ref[off % SUBLANES, lo:lo + rows, :]
                acc = acc + win.reshape(rows // SUBLANES, SUBLANES, CONV_CW) * wb[None]
            y_ref[c, r0:r0 + rows, :] = acc.reshape(rows, CONV_CW)
        return carry

    lax.fori_loop(0, n_chunks, chunk, 0)

    tot = y_ref[0]
    for c in range(1, n_chunks):
        tot = tot + y_ref[c]
    mu = jnp.sum(tot, axis=-1, keepdims=True) * (1.0 / CONV_CH)
    sq = jnp.zeros_like(tot)
    for c in range(n_chunks):
        yc = y_ref[c] - mu
        sq = sq + yc * yc
    var = jnp.sum(sq, axis=-1, keepdims=True) * (1.0 / CONV_CH)
    rstd = lax.rsqrt(var + LN_EPS)
    for c in range(n_chunks):
        cs = slice(c * CONV_CW, (c + 1) * CONV_CW)
        z = (y_ref[c] - mu) * rstd * g_ref[:, cs] + beta_ref[:, cs]
        o_ref[:, cs] = (z * jax.nn.sigmoid(z)).astype(o_ref.dtype)


def _conv_module(u, conv_w, conv_b, ln_g, ln_b, n_seq, seq_len):
    t = u.shape[0]
    ts = CONV_TS
    n_lb = seq_len // ts
    r = ts // CONV_HALO
    n_halo_blocks = t // CONV_HALO
    n_chunks = CONV_CH // CONV_CW
    conv_w = conv_w.reshape(CONV_WIDTH, n_chunks, CONV_CW).transpose(1, 0, 2)
    conv_b = conv_b.reshape(n_chunks, 1, CONV_CW)

    def cur_map(b, lb):
        return (b * n_lb + lb, 0)

    def prev_map(b, lb):
        return (jnp.maximum((b * n_lb + lb) * r - 1, 0), 0)

    def next_map(b, lb):
        return (jnp.minimum((b * n_lb + lb + 1) * r, n_halo_blocks - 1), 0)

    vec = pl.BlockSpec((1, CONV_CH), lambda b, lb: (0, 0))
    blk = _nbytes((ts + 2 * CONV_HALO, CONV_CH), F32) + _nbytes((ts, CONV_CH), BF16)
    return pl.pallas_call(
        _conv_body,
        out_shape=jax.ShapeDtypeStruct((t, CONV_CH), BF16),
        grid=(n_seq, n_lb),
        in_specs=[
            pl.BlockSpec((CONV_HALO, CONV_CH), prev_map),
            pl.BlockSpec((ts, CONV_CH), cur_map),
            pl.BlockSpec((CONV_HALO, CONV_CH), next_map),
            pl.BlockSpec((n_chunks, CONV_WIDTH, CONV_CW), lambda b, lb: (0, 0, 0)),
            pl.BlockSpec((n_chunks, 1, CONV_CW), lambda b, lb: (0, 0, 0)),
            vec, vec,
        ],
        out_specs=pl.BlockSpec((ts, CONV_CH), cur_map),
        scratch_shapes=[pltpu.VMEM((n_chunks, ts + 2 * CONV_HALO, CONV_CW), F32),
                        pltpu.VMEM((n_chunks, ts, CONV_CW), F32),
                        pltpu.VMEM((SUBLANES, ts + 2 * CONV_HALO - SUBLANES, CONV_CW), F32)],
        compiler_params=pltpu.CompilerParams(
            dimension_semantics=("parallel", "arbitrary"),
            vmem_limit_bytes=_vmem_limit(blk, 10 * _nbytes((ts + 2 * CONV_HALO, CONV_CH), F32))),
        name="conv_module",
    )(u, u, u, conv_w, conv_b, ln_g, ln_b)


ATT_BQ = 512
ATT_SUB = 128


def _attn_body(q_ref, kp_ref, kc_ref, kn_ref, vp_ref, vc_ref, vn_ref, o_ref, lse_ref, k_buf, v_buf):
    lb = pl.program_id(1)
    n_lb = pl.num_programs(1)
    bq = q_ref.shape[0]
    sub = min(ATT_SUB, bq)
    n_sub = bq // sub
    nk = sub + 2 * RADIUS
    k_buf[0:RADIUS, :] = kp_ref[...]
    k_buf[RADIUS:RADIUS + bq, :] = kc_ref[...]
    k_buf[RADIUS + bq:, :] = kn_ref[...]
    v_buf[0:RADIUS, :] = vp_ref[...]
    v_buf[RADIUS:RADIUS + bq, :] = vc_ref[...]
    v_buf[RADIUS + bq:, :] = vn_ref[...]

    qi = lax.broadcasted_iota(jnp.int32, (sub, nk), 0)
    kt = lax.broadcasted_iota(jnp.int32, (sub, nk), 1)
    band = (kt >= qi) & (kt <= qi + 2 * RADIUS)
    first = band & ((kt >= RADIUS) | (lb > 0))
    last = band & ((kt < RADIUS + sub) | (lb < n_lb - 1))
    scale = 1.0 / math.sqrt(HEAD_DIM)

    for sb in range(n_sub):
        valid = band
        if sb == 0:
            valid = first
        if sb == n_sub - 1:
            valid = valid & last
        qs = slice(sb * sub, (sb + 1) * sub)
        ks = slice(sb * sub, sb * sub + nk)
        for h in range(N_SLOTS):
            hs = slice(h * HEAD_DIM, (h + 1) * HEAD_DIM)
            s = lax.dot_general(q_ref[qs, hs], k_buf[ks, hs], (((1,), (1,)), ((), ())),
                                preferred_element_type=F32)
            s = jnp.where(valid, s * scale, NEG_INF)
            m = jnp.max(s, axis=-1, keepdims=True)
            e = jnp.exp(s - m)
            den = jnp.sum(e, axis=-1, keepdims=True)
            pv = jnp.dot(e.astype(BF16), v_buf[ks, hs], preferred_element_type=F32)
            o_ref[qs, hs] = (pv / den).astype(o_ref.dtype)
            lse_ref[qs, hs] = jnp.broadcast_to(m + jnp.log(den), (sub, HEAD_DIM))


def _attention_group(qkv, dilation, n_seq, seq_len):
    _, d, rows, _ = qkv.shape
    sub_len = seq_len // d
    bq = min(ATT_BQ, sub_len)
    n_lb = sub_len // bq
    hr = bq // RADIUS
    n_halo = rows // RADIUS

    def cur(which):
        return pl.BlockSpec((None, None, bq, GROUP_W),
                            lambda b, lb, r: (which, r, b * n_lb + lb, 0))

    def halo_prev(which):
        return pl.BlockSpec((None, None, RADIUS, GROUP_W),
                            lambda b, lb, r: (which, r, jnp.maximum((b * n_lb + lb) * hr - 1, 0), 0))

    def halo_next(which):
        return pl.BlockSpec((None, None, RADIUS, GROUP_W),
                            lambda b, lb, r: (which, r, jnp.minimum((b * n_lb + lb + 1) * hr, n_halo - 1), 0))

    out_spec = pl.BlockSpec((None, bq, GROUP_W), lambda b, lb, r: (r, b * n_lb + lb, 0))
    nk = bq + 2 * RADIUS
    blk = 3 * _nbytes((nk, GROUP_W), BF16) + 2 * _nbytes((bq, GROUP_W), F32)
    return pl.pallas_call(
        _attn_body,
        out_shape=[jax.ShapeDtypeStruct((d, rows, GROUP_W), BF16),
                   jax.ShapeDtypeStruct((d, rows, GROUP_W), F32)],
        grid=(n_seq, n_lb, d),
        in_specs=[cur(0), halo_prev(1), cur(1), halo_next(1), halo_prev(2), cur(2), halo_next(2)],
        out_specs=[out_spec, out_spec],
        scratch_shapes=[pltpu.VMEM((nk, GROUP_W), BF16), pltpu.VMEM((nk, GROUP_W), BF16)],
        compiler_params=pltpu.CompilerParams(
            dimension_semantics=("parallel", "arbitrary", "arbitrary"),
            vmem_limit_bytes=_vmem_limit(blk, 2 * _nbytes((nk, GROUP_W), BF16) + (8 << 20))),
        name=f"band_attention_d{d}",
    )(qkv, qkv, qkv, qkv, qkv, qkv, qkv)


COMBINE_ROWS = 512


def _combine_body(*refs):
    o_refs = refs[0:N_GROUPS]
    l_refs = refs[N_GROUPS:2 * N_GROUPS]
    out_ref = refs[2 * N_GROUPS]
    scratch = list(refs[2 * N_GROUPS + 1:])

    def position_major(ref):
        d, n, w = ref.shape
        if d == 1:
            return ref[0].astype(F32)
        buf = scratch.pop()
        for c in range(w // LANES):
            for r in range(d):
                buf[c, pl.ds(r, n, stride=d), :] = ref[r, :, c * LANES:(c + 1) * LANES].astype(F32)
        return jnp.concatenate([buf[c] for c in range(w // LANES)], axis=1)

    os_ = [position_major(r) for r in o_refs]
    ls = [position_major(r) for r in l_refs]
    m = functools.reduce(jnp.maximum, ls)
    es = [jnp.exp(l - m) for l in ls]
    num = sum(e * o for e, o in zip(es, os_))
    out_ref[...] = (num / sum(es)).astype(out_ref.dtype)


def _combine_groups(os_, lses):
    t = os_[0].shape[0] * os_[0].shape[1]
    tr = COMBINE_ROWS
    specs = [pl.BlockSpec((a.shape[0], tr // a.shape[0], GROUP_W), lambda i: (0, i, 0))
             for a in (*os_, *lses)]
    n_scratch = sum(1 for a in (*os_, *lses) if a.shape[0] > 1)
    return pl.pallas_call(
        _combine_body,
        out_shape=jax.ShapeDtypeStruct((t, GROUP_W), BF16),
        grid=(t // tr,),
        in_specs=specs,
        out_specs=pl.BlockSpec((tr, GROUP_W), lambda i: (i, 0)),
        scratch_shapes=[pltpu.VMEM((GROUP_W // LANES, tr, LANES), F32)] * n_scratch,
        compiler_params=pltpu.CompilerParams(
            dimension_semantics=("parallel",),
            vmem_limit_bytes=_vmem_limit(7 * _nbytes((tr, GROUP_W), F32),
                                         (n_scratch + 6) * _nbytes((tr, GROUP_W), F32))),
        name="combine_groups",
    )(*os_, *lses)


def _ln_body(x_ref, g_ref, b_ref, *out_refs):
    x = x_ref[...]
    mu = jnp.mean(x, axis=-1, keepdims=True)
    xc = x - mu
    var = jnp.mean(xc * xc, axis=-1, keepdims=True)
    y = xc * lax.rsqrt(var + LN_EPS) * g_ref[...] + b_ref[...]
    for o in out_refs:
        o[...] = y.astype(o.dtype)


def _layer_norm(name, x, g, b, out_dtypes):
    t, dm = x.shape
    tr = 256
    spec = pl.BlockSpec((tr, dm), lambda i: (i, 0))
    vec = pl.BlockSpec((1, dm), lambda i: (0, 0))
    return pl.pallas_call(
        _ln_body,
        out_shape=[jax.ShapeDtypeStruct((t, dm), dt) for dt in out_dtypes],
        grid=(t // tr,),
        in_specs=[spec, vec, vec],
        out_specs=[spec] * len(out_dtypes),
        compiler_params=pltpu.CompilerParams(
            dimension_semantics=("parallel",),
            vmem_limit_bytes=_vmem_limit((1 + len(out_dtypes)) * _nbytes((tr, dm), F32),
                                         4 * _nbytes((tr, dm), F32))),
        name=name,
    )(x, g, b)


def _rope_tables(seq_len):
    pos = jnp.arange(seq_len, dtype=F32)
    inv = ROPE_THETA ** (-jnp.arange(0, ROPE_DIM, 2, dtype=F32) / ROPE_DIM)
    ang = pos[:, None] * inv[None, :]
    cos, sin = jnp.cos(ang), jnp.sin(ang)
    pad1 = jnp.ones((seq_len, HEAD_DIM - ROPE_DIM), F32)
    pad0 = jnp.zeros((seq_len, HEAD_DIM - ROPE_DIM), F32)
    cos_t = jnp.concatenate([cos, cos, pad1], axis=1)
    sin_t = jnp.concatenate([-sin, sin, pad0], axis=1)
    return (jnp.stack([cos_t, jnp.ones_like(cos_t)]), jnp.stack([sin_t, jnp.zeros_like(sin_t)]))


def _encoder_layer(x, p, wts, n_seq, seq_len):
    t = x.shape[0]
    tm = MM_TM
    xb = x.astype(BF16)

    def tile(rows, width):
        return ((rows, width), lambda i, j: (i, j))

    w_in = wts["w_in"]
    o_q = 2 * CONV_CH
    o_gate = o_q + 3 * ATTN_W
    tg = min(MM_TN // 2, CONV_CH)
    (u,) = _matmul(
        "proj_glu", [xb], [(0, w_in, _cols(0)), (0, w_in, _cols(CONV_CH // tg))], [],
        [(jax.ShapeDtypeStruct((t, CONV_CH), F32), *tile(tm, tg))],
        _ep_glu, tm=tm, tn=tg, n_j=CONV_CH // tg)

    cos_t, sin_t = _rope_tables(seq_len)
    pos_blocks = seq_len // tm
    tq = GROUP_W
    tab = ((None, tm, HEAD_DIM), lambda i, j: (j // 2, i % pos_blocks, 0))
    qkv_groups = []
    for g, d in enumerate(DILATIONS):
        col = lambda j, g=g: o_q // tq + j * N_GROUPS + g
        (qkv_g,) = _matmul(
            f"proj_qkv_d{d}", [xb], [(0, w_in, col)], [(cos_t, *tab), (sin_t, *tab)],
            [(jax.ShapeDtypeStruct((3, d, t // d, GROUP_W), BF16), (None, d, tm // d, tq),
              lambda i, j: (j, 0, i, 0))],
            functools.partial(_ep_qkv, heads_per_block=tq // HEAD_DIM, dilation=d),
            tm=tm, tn=tq, n_j=3,
            scratch=() if d == 1 else (pltpu.VMEM((tq // LANES, tm // M_SPLIT, LANES), F32),))
        qkv_groups.append(qkv_g)

    tw = min(MM_TN, D_MODEL)
    n_dj = D_MODEL // tw
    (gates,) = _matmul(
        "proj_gates", [xb], [(0, w_in, _cols(o_gate // tw))], [],
        [(jax.ShapeDtypeStruct((t, 2 * D_MODEL), BF16), *tile(tm, tw))],
        _ep_sigmoid, tm=tm, tn=tw, n_j=2 * n_dj)

    u2 = _conv_module(u, wts["conv_w"], wts["conv_b"], wts["conv_ln_g"], wts["conv_ln_b"],
                      n_seq, seq_len)

    os_, lses = [], []
    for qkv_g, d in zip(qkv_groups, DILATIONS):
        o_g, lse_g = _attention_group(qkv_g, d, n_seq, seq_len)
        os_.append(o_g)
        lses.append(lse_g)
    attn = _combine_groups(os_, lses)

    (merged,) = _matmul(
        "merge", [u2, attn], [(0, wts["w_conv_out"], _cols(0)), (1, wts["w_attn_out"], _cols(0))],
        [(gates, *tile(tm, tw)), (gates, (tm, tw), lambda i, j: (i, n_dj + j))],
        [(jax.ShapeDtypeStruct((t, D_MODEL), BF16), *tile(tm, tw))],
        _ep_merge, tm=tm, tn=tw, n_j=n_dj, m_split=MERGE_M_SPLIT)
    tr = min(MM_TN_F32, D_MODEL)
    (pre1,) = _matmul(
        "out_proj", [merged], [(0, wts["w_out"], _cols(0))], [(x, *tile(tm, tr))],
        [(jax.ShapeDtypeStruct((t, D_MODEL), F32), *tile(tm, tr))],
        functools.partial(_ep_residual, scale=ALPHA), tm=tm, tn=tr, n_j=D_MODEL // tr)
    x1, x1b = _layer_norm("ln1", pre1, wts["ln1_g"], wts["ln1_b"], (F32, BF16))

    tf = FFN_UP_TN
    (act,) = _matmul(
        "ffn_up", [x1b], [(0, wts["w_gate_up"], _cols(0)), (0, wts["w_gate_up"], _cols(D_FF // tf))],
        [], [(jax.ShapeDtypeStruct((t, D_FF), BF16), *tile(FFN_UP_TM, tf))],
        _ep_swiglu, tm=FFN_UP_TM, tn=tf, n_j=D_FF // tf, lhs_single_buffer=True)
    pb = p.astype(BF16)
    (res2,) = _matmul(
        "ple", [x1b, pb], [(0, wts["w_ple_gate"], _cols(0)), (1, wts["w_ple_proj"], _cols(0))],
        [(x1, *tile(tm, tr))],
        [(jax.ShapeDtypeStruct((t, D_MODEL), F32), *tile(tm, tr))],
        _ep_ple, tm=tm, tn=tr, n_j=D_MODEL // tr)
    td_n = min(FFN_DOWN_TN, D_MODEL)
    kb = D_FF // FFN_DOWN_K_SPLIT
    pre2 = res2
    for ki in range(FFN_DOWN_K_SPLIT):
        (pre2,) = _matmul(
            f"ffn_down_k{ki}", [act], [(0, wts["w_down"], _cols(0))],
            [(pre2, *tile(tm, td_n))],
            [(jax.ShapeDtypeStruct((t, D_MODEL), F32), *tile(tm, td_n))],
            functools.partial(_ep_residual, scale=1.0), tm=tm, tn=td_n, n_j=D_MODEL // td_n,
            k_block=(kb, ki))
    (y,) = _layer_norm("ln2", pre2, wts["ln2_g"], wts["ln2_b"], (F32,))
    return y


def kernel(x_prompt, x_sample, p_prompt, p_sample, w_in, conv_w, conv_b, conv_ln_g, conv_ln_b,
           w_conv_out, w_attn_out, w_out, ln1_g, ln1_b, w_gate_up, w_down, w_ple_gate, w_ple_proj,
           ln2_g, ln2_b):
    depth = w_in.shape[0]
    layers = []
    for i in range(depth):
        layers.append({
            "w_in": w_in[i].astype(BF16),
            "conv_w": conv_w[i].reshape(CONV_WIDTH, CONV_CH),
            "conv_b": conv_b[i].reshape(1, CONV_CH),
            "conv_ln_g": conv_ln_g[i].reshape(1, CONV_CH),
            "conv_ln_b": conv_ln_b[i].reshape(1, CONV_CH),
            "w_conv_out": w_conv_out[i].astype(BF16),
            "w_attn_out": w_attn_out[i].astype(BF16),
            "w_out": w_out[i].astype(BF16),
            "ln1_g": ln1_g[i].reshape(1, D_MODEL),
            "ln1_b": ln1_b[i].reshape(1, D_MODEL),
            "w_gate_up": w_gate_up[i],
            "w_down": w_down[i].astype(BF16),
            "w_ple_gate": w_ple_gate[i].astype(BF16),
            "w_ple_proj": w_ple_proj[i].astype(BF16),
            "ln2_g": ln2_g[i].reshape(1, D_MODEL),
            "ln2_b": ln2_b[i].reshape(1, D_MODEL),
        })
    outs = []
    for x, p in ((x_prompt, p_prompt), (x_sample, p_sample)):
        n_seq, seq_len, _ = x.shape
        h = x.reshape(n_seq * seq_len, D_MODEL)
        for i, wts in enumerate(layers):
            h = _encoder_layer(h, p[i].reshape(n_seq * seq_len, PLE_DIM), wts, n_seq, seq_len)
        outs.append(h.reshape(n_seq, seq_len, D_MODEL))
    return tuple(outs)
```

```python
import functools
import math

import jax
import jax.numpy as jnp
from jax import lax
from jax.experimental import pallas as pl
from jax.experimental.pallas import tpu as pltpu

D_MODEL = 4096
CONV_CH = 2048
CONV_WIDTH = 31
CONV_HALF = CONV_WIDTH // 2
HEAD_DIM = 128
N_SLOTS = 8
DILATIONS = (1, 4, 16)
RADIUS = 64
N_GROUPS = len(DILATIONS)
GROUP_W = N_SLOTS * HEAD_DIM
ATTN_W = N_GROUPS * GROUP_W
ROPE_DIM = HEAD_DIM // 4
ROPE_HALF = ROPE_DIM // 2
ROPE_THETA = 500000.0
D_FF = 11008
PLE_DIM = 256
ALPHA = 2.0 ** 0.25
LN_EPS = 1e-5
NEG_INF = -1e30

V7X_VMEM_LIMIT_CAP = 56 * 1024 * 1024
M_SPLIT = 4
MERGE_M_SPLIT = 2
MM_TM = 1024
MM_TN = 1024
MM_TN_F32 = 1024
FFN_UP_TM, FFN_UP_TN = 2048, 256
FFN_DOWN_TN = 512
FFN_DOWN_K_SPLIT = 2
LANES = 128
SUBLANES = 8
HEAD_LANES = LANES // N_SLOTS

F32 = jnp.float32
BF16 = jnp.bfloat16


def _nbytes(shape, dtype):
    return math.prod(shape) * jnp.dtype(dtype).itemsize


def _vmem_limit(block_bytes, temp_bytes):
    est = 2 * block_bytes + temp_bytes + (4 << 20)
    return int(min(max(est, 16 << 20), V7X_VMEM_LIMIT_CAP))


def _mm_body(*refs, n_lhs, dots, n_extra, n_out, m_split, epilogue):
    lhs = refs[:n_lhs]
    rhs = refs[n_lhs:n_lhs + len(dots)]
    extra = refs[n_lhs + len(dots):n_lhs + len(dots) + n_extra]
    outs = refs[n_lhs + len(dots) + n_extra:n_lhs + len(dots) + n_extra + n_out]
    scratch = refs[n_lhs + len(dots) + n_extra + n_out:]
    ws = [None if r.dtype == BF16 else r[...].astype(BF16) for r in rhs]
    rows = lhs[0].shape[0] // m_split
    for c in range(m_split):
        rs = slice(c * rows, (c + 1) * rows)
        accs = [jnp.dot(lhs[li][rs, :], r[...] if w is None else w, preferred_element_type=F32)
                for li, r, w in zip(dots, rhs, ws)]
        epilogue(rs, accs, extra, outs, scratch)


def _matmul(name, lhs, rhs, extras, outs, epilogue, *, tm, tn, n_j, m_split=M_SPLIT, scratch=(),
            lhs_single_buffer=False, k_block=None):
    m = lhs[0].shape[0]
    grid = (m // tm, n_j)
    in_specs, args, blk = [], [], 0
    once = 0
    for a in lhs:
        mode = dict(pipeline_mode=pl.Buffered(1)) if lhs_single_buffer else {}
        kb, ki = k_block or (a.shape[1], 0)
        in_specs.append(pl.BlockSpec((tm, kb), lambda i, j, ki=ki: (i, ki), **mode))
        args.append(a)
        if lhs_single_buffer:
            once += _nbytes((tm, kb), a.dtype)
        else:
            blk += _nbytes((tm, kb), a.dtype)
    for _, w, col in rhs:
        kb, ki = k_block or (w.shape[0], 0)
        in_specs.append(pl.BlockSpec((kb, tn), lambda i, j, col=col, ki=ki: (ki, col(j))))
        args.append(w)
        blk += _nbytes((kb, tn), w.dtype)
        if w.dtype != BF16:
            once += _nbytes((kb, tn), BF16)
    for a, bs, im in extras:
        in_specs.append(pl.BlockSpec(bs, im))
        args.append(a)
        blk += _nbytes([b for b in bs if b is not None], a.dtype)
    out_shape, out_specs = [], []
    for sds, bs, im in outs:
        out_shape.append(sds)
        out_specs.append(pl.BlockSpec(bs, im))
        blk += _nbytes([b for b in bs if b is not None], sds.dtype)
    temp = ((len(rhs) + 4) * (tm // m_split) * tn * 4 + once
            + sum(_nbytes(s.shape, s.dtype) for s in scratch))
    body = functools.partial(_mm_body, n_lhs=len(lhs), dots=tuple(r[0] for r in rhs),
                             n_extra=len(extras), n_out=len(outs), m_split=m_split, epilogue=epilogue)
    return pl.pallas_call(
        body,
        out_shape=out_shape,
        grid=grid,
        in_specs=in_specs,
        out_specs=out_specs,
        scratch_shapes=list(scratch),
        compiler_params=pltpu.CompilerParams(
            dimension_semantics=("parallel", "arbitrary"),
            vmem_limit_bytes=_vmem_limit(blk, temp)),
        name=name,
    )(*args)


def _cols(offset):
    return lambda j: offset + j


def _ep_glu(rs, accs, extra, outs, scratch):
    a, b = accs
    outs[0][rs, :] = a * jax.nn.sigmoid(b)


def _ep_sigmoid(rs, accs, extra, outs, scratch):
    outs[0][rs, :] = jax.nn.sigmoid(accs[0]).astype(outs[0].dtype)


def _ep_qkv(rs, accs, extra, outs, scratch, *, heads_per_block, dilation):
    acc = accs[0]
    cos_ref, sin_ref = extra
    o_ref = outs[0]
    cos = jnp.concatenate([cos_ref[rs, :]] * heads_per_block, axis=1)
    sin = jnp.concatenate([sin_ref[rs, :]] * heads_per_block, axis=1)
    width = acc.shape[1]
    lane = lax.broadcasted_iota(jnp.int32, acc.shape, 1) % HEAD_DIM
    partner = jnp.where(lane < ROPE_HALF,
                        pltpu.roll(acc, width - ROPE_HALF, axis=1),
                        pltpu.roll(acc, ROPE_HALF, axis=1))
    val = acc * cos + partner * sin
    ro = slice(rs.start // dilation, rs.stop // dilation)
    if dilation == 1:
        o_ref[0, ro, :] = val.astype(o_ref.dtype)
    else:
        stage = scratch[0]
        n = val.shape[0] // dilation
        for c in range(width // LANES):
            cs = slice(c * LANES, (c + 1) * LANES)
            stage[c] = val[:, cs]
            for r in range(dilation):
                o_ref[r, ro, cs] = stage[c, pl.ds(r, n, stride=dilation), :].astype(o_ref.dtype)


def _ep_merge(rs, accs, extra, outs, scratch):
    conv, attn = accs
    gc_ref, ga_ref = extra
    merged = gc_ref[rs, :].astype(F32) * conv + ga_ref[rs, :].astype(F32) * attn
    outs[0][rs, :] = merged.astype(outs[0].dtype)


def _ep_residual(rs, accs, extra, outs, scratch, *, scale):
    outs[0][rs, :] = scale * extra[0][rs, :] + accs[0]


def _ep_swiglu(rs, accs, extra, outs, scratch):
    g, u = accs
    outs[0][rs, :] = (g * jax.nn.sigmoid(g) * u).astype(outs[0].dtype)


def _ep_ple(rs, accs, extra, outs, scratch):
    gate, proj = accs
    outs[0][rs, :] = ALPHA * extra[0][rs, :] + jax.nn.sigmoid(gate) * proj


CONV_TS = 256
CONV_HALO = 16
CONV_CW = 128
CONV_ROWS = 128


def _conv_body(prev_ref, cur_ref, next_ref, w_ref, b_ref, g_ref, beta_ref, o_ref, xp_ref, y_ref, sh_ref):
    lb = pl.program_id(1)
    n_lb = pl.num_programs(1)
    ts = cur_ref.shape[0]
    prev = jnp.where(lb > 0, prev_ref[...], 0.0)
    nxt = jnp.where(lb < n_lb - 1, next_ref[...], 0.0)
    n_chunks = CONV_CH // CONV_CW
    for c in range(n_chunks):
        cs = slice(c * CONV_CW, (c + 1) * CONV_CW)
        xp_ref[c, 0:CONV_HALO, :] = prev[:, cs]
        xp_ref[c, CONV_HALO:CONV_HALO + ts, :] = cur_ref[:, cs]
        xp_ref[c, CONV_HALO + ts:, :] = nxt[:, cs]
    base = CONV_HALO - CONV_HALF
    span = ts + (base + CONV_WIDTH - 1) // SUBLANES * SUBLANES
    rows = min(ts, CONV_ROWS)

    def chunk(c, carry):
        for k in range(SUBLANES):
            sh_ref[k] = xp_ref[c, k:k + span, :]
        w_c = w_ref[c]
        bias = jnp.broadcast_to(b_ref[c], (rows // SUBLANES, SUBLANES, CONV_CW))
        for r0 in range(0, ts, rows):
            acc = bias
            for tap in range(CONV_WIDTH):
                off = base + tap
                lo = off // SUBLANES * SUBLANES + r0
                wb = jnp.broadcast_to(w_c[tap:tap + 1, :], (SUBLANES, CONV_CW))
                win = sh_ref[off % SUBLANES, lo:lo + rows, :]
                acc = acc + win.reshape(rows // SUBLANES, SUBLANES, CONV_CW) * wb[None]
            y_ref[c, r0:r0 + rows, :] = acc.reshape(rows, CONV_CW)
        return carry

    lax.fori_loop(0, n_chunks, chunk, 0)

    tot = y_ref[0]
    for c in range(1, n_chunks):
        tot = tot + y_ref[c]
    mu = jnp.sum(tot, axis=-1, keepdims=True) * (1.0 / CONV_CH)
    sq = jnp.zeros_like(tot)
    for c in range(n_chunks):
        yc = y_ref[c] - mu
        sq = sq + yc * yc
    var = jnp.sum(sq, axis=-1, keepdims=True) * (1.0 / CONV_CH)
    rstd = lax.rsqrt(var + LN_EPS)
    for c in range(n_chunks):
        cs = slice(c * CONV_CW, (c + 1) * CONV_CW)
        z = (y_ref[c] - mu) * rstd * g_ref[:, cs] + beta_ref[:, cs]
        o_ref[:, cs] = (z * jax.nn.sigmoid(z)).astype(o_ref.dtype)


def _conv_module(u, conv_w, conv_b, ln_g, ln_b, n_seq, seq_len):
    t = u.shape[0]
    ts = CONV_TS
    n_lb = seq_len // ts
    r = ts // CONV_HALO
    n_halo_blocks = t // CONV_HALO
    n_chunks = CONV_CH // CONV_CW
    conv_w = conv_w.reshape(CONV_WIDTH, n_chunks, CONV_CW).transpose(1, 0, 2)
    conv_b = conv_b.reshape(n_chunks, 1, CONV_CW)

    def cur_map(b, lb):
        return (b * n_lb + lb, 0)

    def prev_map(b, lb):
        return (jnp.maximum((b * n_lb + lb) * r - 1, 0), 0)

    def next_map(b, lb):
        return (jnp.minimum((b * n_lb + lb + 1) * r, n_halo_blocks - 1), 0)

    vec = pl.BlockSpec((1, CONV_CH), lambda b, lb: (0, 0))
    blk = _nbytes((ts + 2 * CONV_HALO, CONV_CH), F32) + _nbytes((ts, CONV_CH), BF16)
    return pl.pallas_call(
        _conv_body,
        out_shape=jax.ShapeDtypeStruct((t, CONV_CH), BF16),
        grid=(n_seq, n_lb),
        in_specs=[
            pl.BlockSpec((CONV_HALO, CONV_CH), prev_map),
            pl.BlockSpec((ts, CONV_CH), cur_map),
            pl.BlockSpec((CONV_HALO, CONV_CH), next_map),
            pl.BlockSpec((n_chunks, CONV_WIDTH, CONV_CW), lambda b, lb: (0, 0, 0)),
            pl.BlockSpec((n_chunks, 1, CONV_CW), lambda b, lb: (0, 0, 0)),
            vec, vec,
        ],
        out_specs=pl.BlockSpec((ts, CONV_CH), cur_map),
        scratch_shapes=[pltpu.VMEM((n_chunks, ts + 2 * CONV_HALO, CONV_CW), F32),
                        pltpu.VMEM((n_chunks, ts, CONV_CW), F32),
                        pltpu.VMEM((SUBLANES, ts + 2 * CONV_HALO - SUBLANES, CONV_CW), F32)],
        compiler_params=pltpu.CompilerParams(
            dimension_semantics=("parallel", "arbitrary"),
            vmem_limit_bytes=_vmem_limit(blk, 10 * _nbytes((ts + 2 * CONV_HALO, CONV_CH), F32))),
        name="conv_module",
    )(u, u, u, conv_w, conv_b, ln_g, ln_b)


ATT_BQ = 512
ATT_SUB = 128


def _attn_body(q_ref, kp_ref, kc_ref, kn_ref, vp_ref, vc_ref, vn_ref, o_ref, lse_ref, k_buf, v_buf):
    lb = pl.program_id(1)
    n_lb = pl.num_programs(1)
    bq = q_ref.shape[0]
    sub = min(ATT_SUB, bq)
    n_sub = bq // sub
    nk = sub + 2 * RADIUS
    k_buf[0:RADIUS, :] = kp_ref[...]
    k_buf[RADIUS:RADIUS + bq, :] = kc_ref[...]
    k_buf[RADIUS + bq:, :] = kn_ref[...]
    v_buf[0:RADIUS, :] = vp_ref[...]
    v_buf[RADIUS:RADIUS + bq, :] = vc_ref[...]
    v_buf[RADIUS + bq:, :] = vn_ref[...]

    qi = lax.broadcasted_iota(jnp.int32, (sub, nk), 0)
    kt = lax.broadcasted_iota(jnp.int32, (sub, nk), 1)
    band = (kt >= qi) & (kt <= qi + 2 * RADIUS)
    first = band & ((kt >= RADIUS) | (lb > 0))
    last = band & ((kt < RADIUS + sub) | (lb < n_lb - 1))
    scale = 1.0 / math.sqrt(HEAD_DIM)
    lane_head = lax.shift_right_logical(lax.broadcasted_iota(jnp.int32, (sub, LANES), 1),
                                        HEAD_LANES.bit_length() - 1)

    for sb in range(n_sub):
        valid = band
        if sb == 0:
            valid = first
        if sb == n_sub - 1:
            valid = valid & last
        qs = slice(sb * sub, (sb + 1) * sub)
        ks = slice(sb * sub, sb * sub + nk)
        lse_slab = jnp.zeros((sub, LANES), F32)
        for h in range(N_SLOTS):
            hs = slice(h * HEAD_DIM, (h + 1) * HEAD_DIM)
            s = lax.dot_general(q_ref[qs, hs], k_buf[ks, hs], (((1,), (1,)), ((), ())),
                                preferred_element_type=F32)
            s = jnp.where(valid, s * scale, NEG_INF)
            m = jnp.max(s, axis=-1, keepdims=True)
            e = jnp.exp(s - m)
            den = jnp.sum(e, axis=-1, keepdims=True)
            pv = jnp.dot(e.astype(BF16), v_buf[ks, hs], preferred_element_type=F32)
            o_ref[qs, hs] = (pv / den).astype(o_ref.dtype)
            lse_slab = jnp.where(lane_head == h, m + jnp.log(den), lse_slab)
        lse_ref[qs, :] = lse_slab


def _attention_group(qkv, dilation, n_seq, seq_len):
    _, d, rows, _ = qkv.shape
    sub_len = seq_len // d
    bq = min(ATT_BQ, sub_len)
    n_lb = sub_len // bq
    hr = bq // RADIUS
    n_halo = rows // RADIUS

    def cur(which):
        return pl.BlockSpec((None, None, bq, GROUP_W),
                            lambda b, lb, r: (which, r, b * n_lb + lb, 0))

    def halo_prev(which):
        return pl.BlockSpec((None, None, RADIUS, GROUP_W),
                            lambda b, lb, r: (which, r, jnp.maximum((b * n_lb + lb) * hr - 1, 0), 0))

    def halo_next(which):
        return pl.BlockSpec((None, None, RADIUS, GROUP_W),
                            lambda b, lb, r: (which, r, jnp.minimum((b * n_lb + lb + 1) * hr, n_halo - 1), 0))

    out_spec = pl.BlockSpec((None, bq, GROUP_W), lambda b, lb, r: (r, b * n_lb + lb, 0))
    nk = bq + 2 * RADIUS
    blk = 3 * _nbytes((nk, GROUP_W), BF16) + 2 * _nbytes((bq, GROUP_W), F32)
    return pl.pallas_call(
        _attn_body,
        out_shape=[jax.ShapeDtypeStruct((d, rows, GROUP_W), BF16),
                   jax.ShapeDtypeStruct((d, rows, LANES), F32)],
        grid=(n_seq, n_lb, d),
        in_specs=[cur(0), halo_prev(1), cur(1), halo_next(1), halo_prev(2), cur(2), halo_next(2)],
        out_specs=[out_spec, pl.BlockSpec((None, bq, LANES), lambda b, lb, r: (r, b * n_lb + lb, 0))],
        scratch_shapes=[pltpu.VMEM((nk, GROUP_W), BF16), pltpu.VMEM((nk, GROUP_W), BF16)],
        compiler_params=pltpu.CompilerParams(
            dimension_semantics=("parallel", "arbitrary", "arbitrary"),
            vmem_limit_bytes=_vmem_limit(blk, 2 * _nbytes((nk, GROUP_W), BF16) + (8 << 20))),
        name=f"band_attention_d{d}",
    )(qkv, qkv, qkv, qkv, qkv, qkv, qkv)


COMBINE_ROWS = 512


def _combine_body(*refs):
    o_refs = refs[0:N_GROUPS]
    l_refs = refs[N_GROUPS:2 * N_GROUPS]
    expand_ref = refs[2 * N_GROUPS]
    out_ref = refs[2 * N_GROUPS + 1]
    scratch = list(refs[2 * N_GROUPS + 2:])

    def position_major(ref):
        d, n, w = ref.shape
        if d == 1:
            return ref[0].astype(F32)
        buf = scratch.pop(0)
        for c in range(w // LANES):
            for r in range(d):
                buf[c, pl.ds(r, n, stride=d), :] = ref[r, :, c * LANES:(c + 1) * LANES].astype(F32)
        return jnp.concatenate([buf[c] for c in range(w // LANES)], axis=1)

    os_ = [position_major(r) for r in o_refs]
    ls = [position_major(r) for r in l_refs]
    m = functools.reduce(jnp.maximum, ls)
    es = [jnp.exp(l - m) for l in ls]
    inv = 1.0 / sum(es)
    acc = None
    for e, o in zip(es, os_):
        w = jnp.dot((e * inv).astype(BF16), expand_ref[...], preferred_element_type=F32)
        acc = w * o if acc is None else acc + w * o
    out_ref[...] = acc.astype(out_ref.dtype)


def _combine_groups(os_, lses):
    t = os_[0].shape[0] * os_[0].shape[1]
    tr = COMBINE_ROWS
    arrays = (*os_, *lses)
    specs = [pl.BlockSpec((a.shape[0], tr // a.shape[0], a.shape[2]), lambda i: (0, i, 0))
             for a in arrays]
    expand = ((jnp.arange(LANES)[:, None] // HEAD_LANES == jnp.arange(GROUP_W)[None, :] // HEAD_DIM)
              .astype(F32) / HEAD_LANES).astype(BF16)
    specs.append(pl.BlockSpec((LANES, GROUP_W), lambda i: (0, 0)))
    scratch = [pltpu.VMEM((a.shape[2] // LANES, tr, LANES), F32) for a in arrays if a.shape[0] > 1]
    return pl.pallas_call(
        _combine_body,
        out_shape=jax.ShapeDtypeStruct((t, GROUP_W), BF16),
        grid=(t // tr,),
        in_specs=specs,
        out_specs=pl.BlockSpec((tr, GROUP_W), lambda i: (i, 0)),
        scratch_shapes=scratch,
        compiler_params=pltpu.CompilerParams(
            dimension_semantics=("parallel",),
            vmem_limit_bytes=_vmem_limit(5 * _nbytes((tr, GROUP_W), F32),
                                         (len(scratch) + 8) * _nbytes((tr, GROUP_W), F32))),
        name="combine_groups",
    )(*arrays, expand)


def _ln_body(x_ref, g_ref, b_ref, *out_refs):
    x = x_ref[...]
    mu = jnp.mean(x, axis=-1, keepdims=True)
    xc = x - mu
    var = jnp.mean(xc * xc, axis=-1, keepdims=True)
    y = xc * lax.rsqrt(var + LN_EPS) * g_ref[...] + b_ref[...]
    for o in out_refs:
        o[...] = y.astype(o.dtype)


def _layer_norm(name, x, g, b, out_dtypes):
    t, dm = x.shape
    tr = 256
    spec = pl.BlockSpec((tr, dm), lambda i: (i, 0))
    vec = pl.BlockSpec((1, dm), lambda i: (0, 0))
    return pl.pallas_call(
        _ln_body,
        out_shape=[jax.ShapeDtypeStruct((t, dm), dt) for dt in out_dtypes],
        grid=(t // tr,),
        in_specs=[spec, vec, vec],
        out_specs=[spec] * len(out_dtypes),
        compiler_params=pltpu.CompilerParams(
            dimension_semantics=("parallel",),
            vmem_limit_bytes=_vmem_limit((1 + len(out_dtypes)) * _nbytes((tr, dm), F32),
                                         4 * _nbytes((tr, dm), F32))),
        name=name,
    )(x, g, b)


def _rope_tables(seq_len):
    pos = jnp.arange(seq_len, dtype=F32)
    inv = ROPE_THETA ** (-jnp.arange(0, ROPE_DIM, 2, dtype=F32) / ROPE_DIM)
    ang = pos[:, None] * inv[None, :]
    cos, sin = jnp.cos(ang), jnp.sin(ang)
    pad1 = jnp.ones((seq_len, HEAD_DIM - ROPE_DIM), F32)
    pad0 = jnp.zeros((seq_len, HEAD_DIM - ROPE_DIM), F32)
    cos_t = jnp.concatenate([cos, cos, pad1], axis=1)
    sin_t = jnp.concatenate([-sin, sin, pad0], axis=1)
    return (jnp.stack([cos_t, jnp.ones_like(cos_t)]), jnp.stack([sin_t, jnp.zeros_like(sin_t)]))


def _encoder_layer(x, p, wts, n_seq, seq_len):
    t = x.shape[0]
    tm = MM_TM
    xb = x.astype(BF16)

    def tile(rows, width):
        return ((rows, width), lambda i, j: (i, j))

    w_in = wts["w_in"]
    o_q = 2 * CONV_CH
    o_gate = o_q + 3 * ATTN_W
    tg = min(MM_TN // 2, CONV_CH)
    (u,) = _matmul(
        "proj_glu", [xb], [(0, w_in, _cols(0)), (0, w_in, _cols(CONV_CH // tg))], [],
        [(jax.ShapeDtypeStruct((t, CONV_CH), F32), *tile(tm, tg))],
        _ep_glu, tm=tm, tn=tg, n_j=CONV_CH // tg)

    cos_t, sin_t = _rope_tables(seq_len)
    pos_blocks = seq_len // tm
    tq = GROUP_W
    tab = ((None, tm, HEAD_DIM), lambda i, j: (j // 2, i % pos_blocks, 0))
    qkv_groups = []
    for g, d in enumerate(DILATIONS):
        col = lambda j, g=g: o_q // tq + j * N_GROUPS + g
        (qkv_g,) = _matmul(
            f"proj_qkv_d{d}", [xb], [(0, w_in, col)], [(cos_t, *tab), (sin_t, *tab)],
            [(jax.ShapeDtypeStruct((3, d, t // d, GROUP_W), BF16), (None, d, tm // d, tq),
              lambda i, j: (j, 0, i, 0))],
            functools.partial(_ep_qkv, heads_per_block=tq // HEAD_DIM, dilation=d),
            tm=tm, tn=tq, n_j=3,
            scratch=() if d == 1 else (pltpu.VMEM((tq // LANES, tm // M_SPLIT, LANES), F32),))
        qkv_groups.append(qkv_g)

    tw = min(MM_TN, D_MODEL)
    n_dj = D_MODEL // tw
    (gates,) = _matmul(
        "proj_gates", [xb], [(0, w_in, _cols(o_gate // tw))], [],
        [(jax.ShapeDtypeStruct((t, 2 * D_MODEL), BF16), *tile(tm, tw))],
        _ep_sigmoid, tm=tm, tn=tw, n_j=2 * n_dj)

    u2 = _conv_module(u, wts["conv_w"], wts["conv_b"], wts["conv_ln_g"], wts["conv_ln_b"],
                      n_seq, seq_len)

    os_, lses = [], []
    for qkv_g, d in zip(qkv_groups, DILATIONS):
        o_g, lse_g = _attention_group(qkv_g, d, n_seq, seq_len)
        os_.append(o_g)
        lses.append(lse_g)
    attn = _combine_groups(os_, lses)

    (merged,) = _matmul(
        "merge", [u2, attn], [(0, wts["w_conv_out"], _cols(0)), (1, wts["w_attn_out"], _cols(0))],
        [(gates, *tile(tm, tw)), (gates, (tm, tw), lambda i, j: (i, n_dj + j))],
        [(jax.ShapeDtypeStruct((t, D_MODEL), BF16), *tile(tm, tw))],
        _ep_merge, tm=tm, tn=tw, n_j=n_dj, m_split=MERGE_M_SPLIT)
    tr = min(MM_TN_F32, D_MODEL)
    (pre1,) = _matmul(
        "out_proj", [merged], [(0, wts["w_out"], _cols(0))], [(x, *tile(tm, tr))],
        [(jax.ShapeDtypeStruct((t, D_MODEL), F32), *tile(tm, tr))],
        functools.partial(_ep_residual, scale=ALPHA), tm=tm, tn=tr, n_j=D_MODEL // tr)
    x1, x1b = _layer_norm("ln1", pre1, wts["ln1_g"], wts["ln1_b"], (F32, BF16))

    tf = FFN_UP_TN
    (act,) = _matmul(
        "ffn_up", [x1b], [(0, wts["w_gate_up"], _cols(0)), (0, wts["w_gate_up"], _cols(D_FF // tf))],
        [], [(jax.ShapeDtypeStruct((t, D_FF), BF16), *tile(FFN_UP_TM, tf))],
        _ep_swiglu, tm=FFN_UP_TM, tn=tf, n_j=D_FF // tf, lhs_single_buffer=True)
    pb = p.astype(BF16)
    (res2,) = _matmul(
        "ple", [x1b, pb], [(0, wts["w_ple_gate"], _cols(0)), (1, wts["w_ple_proj"], _cols(0))],
        [(x1, *tile(tm, tr))],
        [(jax.ShapeDtypeStruct((t, D_MODEL), F32), *tile(tm, tr))],
        _ep_ple, tm=tm, tn=tr, n_j=D_MODEL // tr)
    td_n = min(FFN_DOWN_TN, D_MODEL)
    kb = D_FF // FFN_DOWN_K_SPLIT
    pre2 = res2
    for ki in range(FFN_DOWN_K_SPLIT):
        (pre2,) = _matmul(
            f"ffn_down_k{ki}", [act], [(0, wts["w_down"], _cols(0))],
            [(pre2, *tile(tm, td_n))],
            [(jax.ShapeDtypeStruct((t, D_MODEL), F32), *tile(tm, td_n))],
            functools.partial(_ep_residual, scale=1.0), tm=tm, tn=td_n, n_j=D_MODEL // td_n,
            k_block=(kb, ki))
    (y,) = _layer_norm("ln2", pre2, wts["ln2_g"], wts["ln2_b"], (F32,))
    return y


def kernel(x_prompt, x_sample, p_prompt, p_sample, w_in, conv_w, conv_b, conv_ln_g, conv_ln_b,
           w_conv_out, w_attn_out, w_out, ln1_g, ln1_b, w_gate_up, w_down, w_ple_gate, w_ple_proj,
           ln2_g, ln2_b):
    depth = w_in.shape[0]
    layers = []
    for i in range(depth):
        layers.append({
            "w_in": w_in[i].astype(BF16),
            "conv_w": conv_w[i].reshape(CONV_WIDTH, CONV_CH),
            "conv_b": conv_b[i].reshape(1, CONV_CH),
            "conv_ln_g": conv_ln_g[i].reshape(1, CONV_CH),
            "conv_ln_b": conv_ln_b[i].reshape(1, CONV_CH),
            "w_conv_out": w_conv_out[i].astype(BF16),
            "w_attn_out": w_attn_out[i].astype(BF16),
            "w_out": w_out[i].astype(BF16),
            "ln1_g": ln1_g[i].reshape(1, D_MODEL),
            "ln1_b": ln1_b[i].reshape(1, D_MODEL),
            "w_gate_up": w_gate_up[i],
            "w_down": w_down[i].astype(BF16),
            "w_ple_gate": w_ple_gate[i].astype(BF16),
            "w_ple_proj": w_ple_proj[i].astype(BF16),
            "ln2_g": ln2_g[i].reshape(1, D_MODEL),
            "ln2_b": ln2_b[i].reshape(1, D_MODEL),
        })
    outs = []
    for x, p in ((x_prompt, p_prompt), (x_sample, p_sample)):
        n_seq, seq_len, _ = x.shape
        h = x.reshape(n_seq * seq_len, D_MODEL)
        for i, wts in enumerate(layers):
            h = _encoder_layer(h, p[i].reshape(n_seq * seq_len, PLE_DIM), wts, n_seq, seq_len)
        outs.append(h.reshape(n_seq, seq_len, D_MODEL))
    return tuple(outs)
```

```python
import functools
import math

import jax
import jax.numpy as jnp
from jax import lax
from jax.experimental import pallas as pl
from jax.experimental.pallas import tpu as pltpu

D_MODEL = 4096
CONV_CH = 2048
CONV_WIDTH = 31
CONV_HALF = CONV_WIDTH // 2
HEAD_DIM = 128
N_SLOTS = 8
DILATIONS = (1, 4, 16)
RADIUS = 64
N_GROUPS = len(DILATIONS)
GROUP_W = N_SLOTS * HEAD_DIM
ATTN_W = N_GROUPS * GROUP_W
ROPE_DIM = HEAD_DIM // 4
ROPE_HALF = ROPE_DIM // 2
ROPE_THETA = 500000.0
D_FF = 11008
PLE_DIM = 256
ALPHA = 2.0 ** 0.25
LN_EPS = 1e-5
NEG_INF = -1e30

V7X_VMEM_LIMIT_CAP = 56 * 1024 * 1024
M_SPLIT = 4
MERGE_M_SPLIT = 2
MM_TM = 1024
GLU_TM = 512
MM_TN = 1024
MM_TN_F32 = 1024
FFN_UP_TM, FFN_UP_TN = 2048, 256
FFN_DOWN_TN = 512
FFN_DOWN_K_SPLIT = 2
LANES = 128
SUBLANES = 8
HEAD_LANES = LANES // N_SLOTS

F32 = jnp.float32
BF16 = jnp.bfloat16


def _nbytes(shape, dtype):
    return math.prod(shape) * jnp.dtype(dtype).itemsize


def _vmem_limit(block_bytes, temp_bytes):
    est = 2 * block_bytes + temp_bytes + (4 << 20)
    return int(min(max(est, 16 << 20), V7X_VMEM_LIMIT_CAP))


def _mm_body(*refs, n_lhs, dots, n_extra, n_out, m_split, lhs_copy_out, epilogue):
    lhs = refs[:n_lhs]
    rhs = refs[n_lhs:n_lhs + len(dots)]
    extra = refs[n_lhs + len(dots):n_lhs + len(dots) + n_extra]
    outs = refs[n_lhs + len(dots) + n_extra:n_lhs + len(dots) + n_extra + n_out]
    scratch = refs[n_lhs + len(dots) + n_extra + n_out:]
    ws = [None if r.dtype == BF16 else r[...].astype(BF16) for r in rhs]
    rows = lhs[0].shape[0] // m_split
    for c in range(m_split):
        rs = slice(c * rows, (c + 1) * rows)
        xs = [l[rs, :].astype(BF16) for l in lhs]
        if lhs_copy_out is not None:
            outs[lhs_copy_out][rs, :] = xs[0]
        accs = [jnp.dot(xs[li], r[...] if w is None else w, preferred_element_type=F32)
                for li, r, w in zip(dots, rhs, ws)]
        epilogue(rs, accs, extra, outs, scratch)


def _matmul(name, lhs, rhs, extras, outs, epilogue, *, tm, tn, n_j, m_split=M_SPLIT, scratch=(),
            lhs_single_buffer=False, k_block=None, lhs_copy_out=None):
    m = lhs[0].shape[0]
    grid = (m // tm, n_j)
    in_specs, args, blk = [], [], 0
    once = 0
    for a in lhs:
        mode = dict(pipeline_mode=pl.Buffered(1)) if lhs_single_buffer else {}
        kb, ki = k_block or (a.shape[1], 0)
        in_specs.append(pl.BlockSpec((tm, kb), lambda i, j, ki=ki: (i, ki), **mode))
        args.append(a)
        if lhs_single_buffer:
            once += _nbytes((tm, kb), a.dtype)
        else:
            blk += _nbytes((tm, kb), a.dtype)
    for _, w, col in rhs:
        kb, ki = k_block or (w.shape[0], 0)
        in_specs.append(pl.BlockSpec((kb, tn), lambda i, j, col=col, ki=ki: (ki, col(j))))
        args.append(w)
        blk += _nbytes((kb, tn), w.dtype)
        if w.dtype != BF16:
            once += _nbytes((kb, tn), BF16)
    for a, bs, im in extras:
        in_specs.append(pl.BlockSpec(bs, im))
        args.append(a)
        blk += _nbytes([b for b in bs if b is not None], a.dtype)
    out_shape, out_specs = [], []
    for sds, bs, im in outs:
        out_shape.append(sds)
        out_specs.append(pl.BlockSpec(bs, im))
        blk += _nbytes([b for b in bs if b is not None], sds.dtype)
    temp = ((len(rhs) + 4) * (tm // m_split) * tn * 4 + once
            + sum(_nbytes(s.shape, s.dtype) for s in scratch))
    body = functools.partial(_mm_body, n_lhs=len(lhs), dots=tuple(r[0] for r in rhs),
                             n_extra=len(extras), n_out=len(outs), m_split=m_split,
                             lhs_copy_out=lhs_copy_out, epilogue=epilogue)
    return pl.pallas_call(
        body,
        out_shape=out_shape,
        grid=grid,
        in_specs=in_specs,
        out_specs=out_specs,
        scratch_shapes=list(scratch),
        compiler_params=pltpu.CompilerParams(
            dimension_semantics=("parallel", "arbitrary"),
            vmem_limit_bytes=_vmem_limit(blk, temp)),
        name=name,
    )(*args)


def _cols(offset):
    return lambda j: offset + j


def _ep_glu(rs, accs, extra, outs, scratch):
    a, b = accs
    outs[0][rs, :] = a * jax.nn.sigmoid(b)


def _ep_sigmoid(rs, accs, extra, outs, scratch):
    outs[0][rs, :] = jax.nn.sigmoid(accs[0]).astype(outs[0].dtype)


def _ep_qkv(rs, accs, extra, outs, scratch, *, heads_per_block, dilation):
    acc = accs[0]
    cos_ref, sin_ref = extra
    o_ref = outs[0]
    cos = jnp.concatenate([cos_ref[rs, :]] * heads_per_block, axis=1)
    sin = jnp.concatenate([sin_ref[rs, :]] * heads_per_block, axis=1)
    width = acc.shape[1]
    lane = lax.broadcasted_iota(jnp.int32, acc.shape, 1) % HEAD_DIM
    partner = jnp.where(lane < ROPE_HALF,
                        pltpu.roll(acc, width - ROPE_HALF, axis=1),
                        pltpu.roll(acc, ROPE_HALF, axis=1))
    val = acc * cos + partner * sin
    ro = slice(rs.start // dilation, rs.stop // dilation)
    if dilation == 1:
        o_ref[0, ro, :] = val.astype(o_ref.dtype)
    else:
        stage = scratch[0]
        n = val.shape[0] // dilation
        for c in range(width // LANES):
            cs = slice(c * LANES, (c + 1) * LANES)
            stage[c] = val[:, cs]
            for r in range(dilation):
                o_ref[r, ro, cs] = stage[c, pl.ds(r, n, stride=dilation), :].astype(o_ref.dtype)


def _ep_merge(rs, accs, extra, outs, scratch):
    conv, attn = accs
    gc_ref, ga_ref = extra
    merged = gc_ref[rs, :].astype(F32) * conv + ga_ref[rs, :].astype(F32) * attn
    outs[0][rs, :] = merged.astype(outs[0].dtype)


def _ep_residual(rs, accs, extra, outs, scratch, *, scale):
    outs[0][rs, :] = scale * extra[0][rs, :] + accs[0]


def _ep_swiglu(rs, accs, extra, outs, scratch):
    g, u = accs
    outs[0][rs, :] = (g * jax.nn.sigmoid(g) * u).astype(outs[0].dtype)


def _ep_ple(rs, accs, extra, outs, scratch):
    gate, proj = accs
    outs[0][rs, :] = ALPHA * extra[0][rs, :] + jax.nn.sigmoid(gate) * proj


CONV_TS = 256
CONV_HALO = 16
CONV_CW = 128
CONV_ROWS = 128


def _conv_body(prev_ref, cur_ref, next_ref, w_ref, b_ref, g_ref, beta_ref, o_ref, xp_ref, y_ref, sh_ref):
    lb = pl.program_id(1)
    n_lb = pl.num_programs(1)
    ts = cur_ref.shape[0]
    prev = jnp.where(lb > 0, prev_ref[...], 0.0)
    nxt = jnp.where(lb < n_lb - 1, next_ref[...], 0.0)
    n_chunks = CONV_CH // CONV_CW
    for c in range(n_chunks):
        cs = slice(c * CONV_CW, (c + 1) * CONV_CW)
        xp_ref[c, 0:CONV_HALO, :] = prev[:, cs]
        xp_ref[c, CONV_HALO:CONV_HALO + ts, :] = cur_ref[:, cs]
        xp_ref[c, CONV_HALO + ts:, :] = nxt[:, cs]
    base = CONV_HALO - CONV_HALF
    span = ts + (base + CONV_WIDTH - 1) // SUBLANES * SUBLANES
    rows = min(ts, CONV_ROWS)

    def chunk(c, carry):
        for k in range(SUBLANES):
            sh_ref[k] = xp_ref[c, k:k + span, :]
        w_c = w_ref[c]
        bias = jnp.broadcast_to(b_ref[c], (rows // SUBLANES, SUBLANES, CONV_CW))
        for r0 in range(0, ts, rows):
            acc = bias
            for tap in range(CONV_WIDTH):
                off = base + tap
                lo = off // SUBLANES * SUBLANES + r0
                wb = jnp.broadcast_to(w_c[tap:tap + 1, :], (SUBLANES, CONV_CW))
                win = sh_ref[off % SUBLANES, lo:lo + rows, :]
                acc = acc + win.reshape(rows // SUBLANES, SUBLANES, CONV_CW) * wb[None]
            y_ref[c, r0:r0 + rows, :] = acc.reshape(rows, CONV_CW)
        return carry

    lax.fori_loop(0, n_chunks, chunk, 0)

    tot = y_ref[0]
    for c in range(1, n_chunks):
        tot = tot + y_ref[c]
    mu = jnp.sum(tot, axis=-1, keepdims=True) * (1.0 / CONV_CH)
    sq = jnp.zeros_like(tot)
    for c in range(n_chunks):
        yc = y_ref[c] - mu
        sq = sq + yc * yc
    var = jnp.sum(sq, axis=-1, keepdims=True) * (1.0 / CONV_CH)
    rstd = lax.rsqrt(var + LN_EPS)
    for c in range(n_chunks):
        cs = slice(c * CONV_CW, (c + 1) * CONV_CW)
        z = (y_ref[c] - mu) * rstd * g_ref[:, cs] + beta_ref[:, cs]
        o_ref[:, cs] = (z * jax.nn.sigmoid(z)).astype(o_ref.dtype)


def _conv_module(u, conv_w, conv_b, ln_g, ln_b, n_seq, seq_len):
    t = u.shape[0]
    ts = CONV_TS
    n_lb = seq_len // ts
    r = ts // CONV_HALO
    n_halo_blocks = t // CONV_HALO
    n_chunks = CONV_CH // CONV_CW
    conv_w = conv_w.reshape(CONV_WIDTH, n_chunks, CONV_CW).transpose(1, 0, 2)
    conv_b = conv_b.reshape(n_chunks, 1, CONV_CW)

    def cur_map(b, lb):
        return (b * n_lb + lb, 0)

    def prev_map(b, lb):
        return (jnp.maximum((b * n_lb + lb) * r - 1, 0), 0)

    def next_map(b, lb):
        return (jnp.minimum((b * n_lb + lb + 1) * r, n_halo_blocks - 1), 0)

    vec = pl.BlockSpec((1, CONV_CH), lambda b, lb: (0, 0))
    blk = _nbytes((ts + 2 * CONV_HALO, CONV_CH), F32) + _nbytes((ts, CONV_CH), BF16)
    return pl.pallas_call(
        _conv_body,
        out_shape=jax.ShapeDtypeStruct((t, CONV_CH), BF16),
        grid=(n_seq, n_lb),
        in_specs=[
            pl.BlockSpec((CONV_HALO, CONV_CH), prev_map),
            pl.BlockSpec((ts, CONV_CH), cur_map),
            pl.BlockSpec((CONV_HALO, CONV_CH), next_map),
            pl.BlockSpec((n_chunks, CONV_WIDTH, CONV_CW), lambda b, lb: (0, 0, 0)),
            pl.BlockSpec((n_chunks, 1, CONV_CW), lambda b, lb: (0, 0, 0)),
            vec, vec,
        ],
        out_specs=pl.BlockSpec((ts, CONV_CH), cur_map),
        scratch_shapes=[pltpu.VMEM((n_chunks, ts + 2 * CONV_HALO, CONV_CW), F32),
                        pltpu.VMEM((n_chunks, ts, CONV_CW), F32),
                        pltpu.VMEM((SUBLANES, ts + 2 * CONV_HALO - SUBLANES, CONV_CW), F32)],
        compiler_params=pltpu.CompilerParams(
            dimension_semantics=("parallel", "arbitrary"),
            vmem_limit_bytes=_vmem_limit(blk, 10 * _nbytes((ts + 2 * CONV_HALO, CONV_CH), F32))),
        name="conv_module",
    )(u, u, u, conv_w, conv_b, ln_g, ln_b)


ATT_BQ = 512
ATT_SUB = 128


def _attn_body(q_ref, kp_ref, kc_ref, kn_ref, vp_ref, vc_ref, vn_ref, o_ref, lse_ref, k_buf, v_buf):
    lb = pl.program_id(1)
    n_lb = pl.num_programs(1)
    bq = q_ref.shape[0]
    sub = min(ATT_SUB, bq)
    n_sub = bq // sub
    nk = sub + 2 * RADIUS
    k_buf[0:RADIUS, :] = kp_ref[...]
    k_buf[RADIUS:RADIUS + bq, :] = kc_ref[...]
    k_buf[RADIUS + bq:, :] = kn_ref[...]
    v_buf[0:RADIUS, :] = vp_ref[...]
    v_buf[RADIUS:RADIUS + bq, :] = vc_ref[...]
    v_buf[RADIUS + bq:, :] = vn_ref[...]

    qi = lax.broadcasted_iota(jnp.int32, (sub, nk), 0)
    kt = lax.broadcasted_iota(jnp.int32, (sub, nk), 1)
    band = (kt >= qi) & (kt <= qi + 2 * RADIUS)
    first = band & ((kt >= RADIUS) | (lb > 0))
    last = band & ((kt < RADIUS + sub) | (lb < n_lb - 1))
    scale = 1.0 / math.sqrt(HEAD_DIM)
    lane_head = lax.shift_right_logical(lax.broadcasted_iota(jnp.int32, (sub, LANES), 1),
                                        HEAD_LANES.bit_length() - 1)

    for sb in range(n_sub):
        valid = band
        if sb == 0:
            valid = first
        if sb == n_sub - 1:
            valid = valid & last
        qs = slice(sb * sub, (sb + 1) * sub)
        ks = slice(sb * sub, sb * sub + nk)
        lse_slab = jnp.zeros((sub, LANES), F32)
        for h in range(N_SLOTS):
            hs = slice(h * HEAD_DIM, (h + 1) * HEAD_DIM)
            s = lax.dot_general(q_ref[qs, hs], k_buf[ks, hs], (((1,), (1,)), ((), ())),
                                preferred_element_type=F32)
            s = jnp.where(valid, s * scale, NEG_INF)
            m = jnp.max(s, axis=-1, keepdims=True)
            e = jnp.exp(s - m)
            den = jnp.sum(e, axis=-1, keepdims=True)
            pv = jnp.dot(e.astype(BF16), v_buf[ks, hs], preferred_element_type=F32)
            o_ref[qs, hs] = (pv / den).astype(o_ref.dtype)
            lse_slab = jnp.where(lane_head == h, m + jnp.log(den), lse_slab)
        lse_ref[qs, :] = lse_slab


def _attention_group(qkv, dilation, n_seq, seq_len):
    _, d, rows, _ = qkv.shape
    sub_len = seq_len // d
    bq = min(ATT_BQ, sub_len)
    n_lb = sub_len // bq
    hr = bq // RADIUS
    n_halo = rows // RADIUS

    def cur(which):
        return pl.BlockSpec((None, None, bq, GROUP_W),
                            lambda b, lb, r: (which, r, b * n_lb + lb, 0))

    def halo_prev(which):
        return pl.BlockSpec((None, None, RADIUS, GROUP_W),
                            lambda b, lb, r: (which, r, jnp.maximum((b * n_lb + lb) * hr - 1, 0), 0))

    def halo_next(which):
        return pl.BlockSpec((None, None, RADIUS, GROUP_W),
                            lambda b, lb, r: (which, r, jnp.minimum((b * n_lb + lb + 1) * hr, n_halo - 1), 0))

    out_spec = pl.BlockSpec((None, bq, GROUP_W), lambda b, lb, r: (r, b * n_lb + lb, 0))
    nk = bq + 2 * RADIUS
    blk = 3 * _nbytes((nk, GROUP_W), BF16) + 2 * _nbytes((bq, GROUP_W), F32)
    return pl.pallas_call(
        _attn_body,
        out_shape=[jax.ShapeDtypeStruct((d, rows, GROUP_W), BF16),
                   jax.ShapeDtypeStruct((d, rows, LANES), F32)],
        grid=(n_seq, n_lb, d),
        in_specs=[cur(0), halo_prev(1), cur(1), halo_next(1), halo_prev(2), cur(2), halo_next(2)],
        out_specs=[out_spec, pl.BlockSpec((None, bq, LANES), lambda b, lb, r: (r, b * n_lb + lb, 0))],
        scratch_shapes=[pltpu.VMEM((nk, GROUP_W), BF16), pltpu.VMEM((nk, GROUP_W), BF16)],
        compiler_params=pltpu.CompilerParams(
            dimension_semantics=("parallel", "arbitrary", "arbitrary"),
            vmem_limit_bytes=_vmem_limit(blk, 2 * _nbytes((nk, GROUP_W), BF16) + (8 << 20))),
        name=f"band_attention_d{d}",
    )(qkv, qkv, qkv, qkv, qkv, qkv, qkv)


COMBINE_ROWS = 512


def _combine_body(*refs):
    o_refs = refs[0:N_GROUPS]
    l_refs = refs[N_GROUPS:2 * N_GROUPS]
    expand_ref = refs[2 * N_GROUPS]
    out_ref = refs[2 * N_GROUPS + 1]
    scratch = list(refs[2 * N_GROUPS + 2:])

    def position_major(ref):
        d, n, w = ref.shape
        if d == 1:
            return ref[0].astype(F32)
        buf = scratch.pop(0)
        for c in range(w // LANES):
            for r in range(d):
                buf[c, pl.ds(r, n, stride=d), :] = ref[r, :, c * LANES:(c + 1) * LANES].astype(F32)
        return jnp.concatenate([buf[c] for c in range(w // LANES)], axis=1)

    os_ = [position_major(r) for r in o_refs]
    ls = [position_major(r) for r in l_refs]
    m = functools.reduce(jnp.maximum, ls)
    es = [jnp.exp(l - m) for l in ls]
    inv = 1.0 / sum(es)
    acc = None
    for e, o in zip(es, os_):
        w = jnp.dot((e * inv).astype(BF16), expand_ref[...], preferred_element_type=F32)
        acc = w * o if acc is None else acc + w * o
    out_ref[...] = acc.astype(out_ref.dtype)


def _combine_groups(os_, lses):
    t = os_[0].shape[0] * os_[0].shape[1]
    tr = COMBINE_ROWS
    arrays = (*os_, *lses)
    specs = [pl.BlockSpec((a.shape[0], tr // a.shape[0], a.shape[2]), lambda i: (0, i, 0))
             for a in arrays]
    expand = ((jnp.arange(LANES)[:, None] // HEAD_LANES == jnp.arange(GROUP_W)[None, :] // HEAD_DIM)
              .astype(F32) / HEAD_LANES).astype(BF16)
    specs.append(pl.BlockSpec((LANES, GROUP_W), lambda i: (0, 0)))
    scratch = [pltpu.VMEM((a.shape[2] // LANES, tr, LANES), F32) for a in arrays if a.shape[0] > 1]
    return pl.pallas_call(
        _combine_body,
        out_shape=jax.ShapeDtypeStruct((t, GROUP_W), BF16),
        grid=(t // tr,),
        in_specs=specs,
        out_specs=pl.BlockSpec((tr, GROUP_W), lambda i: (i, 0)),
        scratch_shapes=scratch,
        compiler_params=pltpu.CompilerParams(
            dimension_semantics=("parallel",),
            vmem_limit_bytes=_vmem_limit(5 * _nbytes((tr, GROUP_W), F32),
                                         (len(scratch) + 8) * _nbytes((tr, GROUP_W), F32))),
        name="combine_groups",
    )(*arrays, expand)


def _ln_body(x_ref, g_ref, b_ref, *out_refs):
    x = x_ref[...]
    mu = jnp.mean(x, axis=-1, keepdims=True)
    xc = x - mu
    var = jnp.mean(xc * xc, axis=-1, keepdims=True)
    y = xc * lax.rsqrt(var + LN_EPS) * g_ref[...] + b_ref[...]
    for o in out_refs:
        o[...] = y.astype(o.dtype)


def _layer_norm(name, x, g, b, out_dtypes):
    t, dm = x.shape
    tr = 256
    spec = pl.BlockSpec((tr, dm), lambda i: (i, 0))
    vec = pl.BlockSpec((1, dm), lambda i: (0, 0))
    return pl.pallas_call(
        _ln_body,
        out_shape=[jax.ShapeDtypeStruct((t, dm), dt) for dt in out_dtypes],
        grid=(t // tr,),
        in_specs=[spec, vec, vec],
        out_specs=[spec] * len(out_dtypes),
        compiler_params=pltpu.CompilerParams(
            dimension_semantics=("parallel",),
            vmem_limit_bytes=_vmem_limit((1 + len(out_dtypes)) * _nbytes((tr, dm), F32),
                                         4 * _nbytes((tr, dm), F32))),
        name=name,
    )(x, g, b)


def _rope_tables(seq_len):
    pos = jnp.arange(seq_len, dtype=F32)
    inv = ROPE_THETA ** (-jnp.arange(0, ROPE_DIM, 2, dtype=F32) / ROPE_DIM)
    ang = pos[:, None] * inv[None, :]
    cos, sin = jnp.cos(ang), jnp.sin(ang)
    pad1 = jnp.ones((seq_len, HEAD_DIM - ROPE_DIM), F32)
    pad0 = jnp.zeros((seq_len, HEAD_DIM - ROPE_DIM), F32)
    cos_t = jnp.concatenate([cos, cos, pad1], axis=1)
    sin_t = jnp.concatenate([-sin, sin, pad0], axis=1)
    return (jnp.stack([cos_t, jnp.ones_like(cos_t)]), jnp.stack([sin_t, jnp.zeros_like(sin_t)]))


def _encoder_layer(x, p, wts, n_seq, seq_len):
    t = x.shape[0]
    tm = MM_TM

    def tile(rows, width):
        return ((rows, width), lambda i, j: (i, j))

    w_in = wts["w_in"]
    o_q = 2 * CONV_CH
    o_gate = o_q + 3 * ATTN_W
    tg = min(MM_TN // 2, CONV_CH)
    u, xb = _matmul(
        "proj_glu", [x], [(0, w_in, _cols(0)), (0, w_in, _cols(CONV_CH // tg))], [],
        [(jax.ShapeDtypeStruct((t, CONV_CH), F32), *tile(GLU_TM, tg)),
         (jax.ShapeDtypeStruct((t, D_MODEL), BF16), (GLU_TM, D_MODEL), lambda i, j: (i, 0))],
        _ep_glu, tm=GLU_TM, tn=tg, n_j=CONV_CH // tg, lhs_copy_out=1)

    cos_t, sin_t = _rope_tables(seq_len)
    pos_blocks = seq_len // tm
    tq = GROUP_W
    tab = ((None, tm, HEAD_DIM), lambda i, j: (j // 2, i % pos_blocks, 0))
    qkv_groups = []
    for g, d in enumerate(DILATIONS):
        col = lambda j, g=g: o_q // tq + j * N_GROUPS + g
        (qkv_g,) = _matmul(
            f"proj_qkv_d{d}", [xb], [(0, w_in, col)], [(cos_t, *tab), (sin_t, *tab)],
            [(jax.ShapeDtypeStruct((3, d, t // d, GROUP_W), BF16), (None, d, tm // d, tq),
              lambda i, j: (j, 0, i, 0))],
            functools.partial(_ep_qkv, heads_per_block=tq // HEAD_DIM, dilation=d),
            tm=tm, tn=tq, n_j=3,
            scratch=() if d == 1 else (pltpu.VMEM((tq // LANES, tm // M_SPLIT, LANES), F32),))
        qkv_groups.append(qkv_g)

    tw = min(MM_TN, D_MODEL)
    n_dj = D_MODEL // tw
    (gates,) = _matmul(
        "proj_gates", [xb], [(0, w_in, _cols(o_gate // tw))], [],
        [(jax.ShapeDtypeStruct((t, 2 * D_MODEL), BF16), *tile(tm, tw))],
        _ep_sigmoid, tm=tm, tn=tw, n_j=2 * n_dj)

    u2 = _conv_module(u, wts["conv_w"], wts["conv_b"], wts["conv_ln_g"], wts["conv_ln_b"],
                      n_seq, seq_len)

    os_, lses = [], []
    for qkv_g, d in zip(qkv_groups, DILATIONS):
        o_g, lse_g = _attention_group(qkv_g, d, n_seq, seq_len)
        os_.append(o_g)
        lses.append(lse_g)
    attn = _combine_groups(os_, lses)

    (merged,) = _matmul(
        "merge", [u2, attn], [(0, wts["w_conv_out"], _cols(0)), (1, wts["w_attn_out"], _cols(0))],
        [(gates, *tile(tm, tw)), (gates, (tm, tw), lambda i, j: (i, n_dj + j))],
        [(jax.ShapeDtypeStruct((t, D_MODEL), BF16), *tile(tm, tw))],
        _ep_merge, tm=tm, tn=tw, n_j=n_dj, m_split=MERGE_M_SPLIT)
    tr = min(MM_TN_F32, D_MODEL)
    (pre1,) = _matmul(
        "out_proj", [merged], [(0, wts["w_out"], _cols(0))], [(x, *tile(tm, tr))],
        [(jax.ShapeDtypeStruct((t, D_MODEL), F32), *tile(tm, tr))],
        functools.partial(_ep_residual, scale=ALPHA), tm=tm, tn=tr, n_j=D_MODEL // tr)
    x1, x1b = _layer_norm("ln1", pre1, wts["ln1_g"], wts["ln1_b"], (F32, BF16))

    tf = FFN_UP_TN
    (act,) = _matmul(
        "ffn_up", [x1b], [(0, wts["w_gate_up"], _cols(0)), (0, wts["w_gate_up"], _cols(D_FF // tf))],
        [], [(jax.ShapeDtypeStruct((t, D_FF), BF16), *tile(FFN_UP_TM, tf))],
        _ep_swiglu, tm=FFN_UP_TM, tn=tf, n_j=D_FF // tf, lhs_single_buffer=True)
    pb = p.astype(BF16)
    (res2,) = _matmul(
        "ple", [x1b, pb], [(0, wts["w_ple_gate"], _cols(0)), (1, wts["w_ple_proj"], _cols(0))],
        [(x1, *tile(tm, tr))],
        [(jax.ShapeDtypeStruct((t, D_MODEL), F32), *tile(tm, tr))],
        _ep_ple, tm=tm, tn=tr, n_j=D_MODEL // tr)
    td_n = min(FFN_DOWN_TN, D_MODEL)
    kb = D_FF // FFN_DOWN_K_SPLIT
    pre2 = res2
    for ki in range(FFN_DOWN_K_SPLIT):
        (pre2,) = _matmul(
            f"ffn_down_k{ki}", [act], [(0, wts["w_down"], _cols(0))],
            [(pre2, *tile(tm, td_n))],
            [(jax.ShapeDtypeStruct((t, D_MODEL), F32), *tile(tm, td_n))],
            functools.partial(_ep_residual, scale=1.0), tm=tm, tn=td_n, n_j=D_MODEL // td_n,
            k_block=(kb, ki))
    (y,) = _layer_norm("ln2", pre2, wts["ln2_g"], wts["ln2_b"], (F32,))
    return y


def kernel(x_prompt, x_sample, p_prompt, p_sample, w_in, conv_w, conv_b, conv_ln_g, conv_ln_b,
           w_conv_out, w_attn_out, w_out, ln1_g, ln1_b, w_gate_up, w_down, w_ple_gate, w_ple_proj,
           ln2_g, ln2_b):
    depth = w_in.shape[0]
    layers = []
    for i in range(depth):
        layers.append({
            "w_in": w_in[i].astype(BF16),
            "conv_w": conv_w[i].reshape(CONV_WIDTH, CONV_CH),
            "conv_b": conv_b[i].reshape(1, CONV_CH),
            "conv_ln_g": conv_ln_g[i].reshape(1, CONV_CH),
            "conv_ln_b": conv_ln_b[i].reshape(1, CONV_CH),
            "w_conv_out": w_conv_out[i].astype(BF16),
            "w_attn_out": w_attn_out[i].astype(BF16),
            "w_out": w_out[i].astype(BF16),
            "ln1_g": ln1_g[i].reshape(1, D_MODEL),
            "ln1_b": ln1_b[i].reshape(1, D_MODEL),
            "w_gate_up": w_gate_up[i],
            "w_down": w_down[i].astype(BF16),
            "w_ple_gate": w_ple_gate[i].astype(BF16),
            "w_ple_proj": w_ple_proj[i].astype(BF16),
            "ln2_g": ln2_g[i].reshape(1, D_MODEL),
            "ln2_b": ln2_b[i].reshape(1, D_MODEL),
        })
    outs = []
    for x, p in ((x_prompt, p_prompt), (x_sample, p_sample)):
        n_seq, seq_len, _ = x.shape
        h = x.reshape(n_seq * seq_len, D_MODEL)
        for i, wts in enumerate(layers):
            h = _encoder_layer(h, p[i].reshape(n_seq * seq_len, PLE_DIM), wts, n_seq, seq_len)
        outs.append(h.reshape(n_seq, seq_len, D_MODEL))
    return tuple(outs)
```
